```python
import math
import jax, jax.numpy as jnp
from jax import lax
import numpy as np

D_MODEL = 1024
BATCH = 16
SEQ = 2048
DEPTH = 4

N_MIXERS = 4
GROUP_W = D_MODEL // N_MIXERS
HEAD_DIM = 64
M_HEADS = GROUP_W // HEAD_DIM
G_HEADS = GROUP_W // HEAD_DIM
CHUNK = 64
GDN_CONV = 4
CONF_CONV = 31
SC_CONV = 3
D_FF = -(-8 * D_MODEL // (3 * 256)) * 256
EPS = 1e-6
PROJ_SECTIONS = (GROUP_W, GROUP_W, GROUP_W, GROUP_W, M_HEADS, M_HEADS,
                 GROUP_W, GROUP_W, GROUP_W, GROUP_W, G_HEADS, G_HEADS,
                 GROUP_W, GROUP_W,
                 GROUP_W, GROUP_W, GROUP_W)
PROJ_W = 13 * GROUP_W + 2 * M_HEADS + 2 * G_HEADS

kernel_name = "hybrid_parallel_heads_mlstm_gdn_conformer_shortconv"


def rmsnorm(x, g):
    xf = x.astype(jnp.float32)
    y = xf * lax.rsqrt(jnp.mean(xf * xf, axis=-1, keepdims=True) + EPS)
    return (y * g.astype(jnp.float32)).astype(x.dtype)


def layernorm(x, g, b):
    xf = x.astype(jnp.float32)
    mu = jnp.mean(xf, axis=-1, keepdims=True)
    xc = xf - mu
    y = xc * lax.rsqrt(jnp.mean(xc * xc, axis=-1, keepdims=True) + EPS)
    return (y * g.astype(jnp.float32) + b.astype(jnp.float32)).astype(x.dtype)


def l2norm(x):
    return x * lax.rsqrt(jnp.sum(x * x, axis=-1, keepdims=True) + EPS)


def causal_dwconv(x, w):
    K, C = w.shape
    return lax.conv_general_dilated(x, w[:, None, :].astype(x.dtype), window_strides=(1,),
                                    padding=[(K - 1, 0)], dimension_numbers=('NWC', 'WIO', 'NWC'),
                                    feature_group_count=C)


def _to_chunks(a):
    b, t = a.shape[:2]
    a = a.reshape(b, t // CHUNK, CHUNK, *a.shape[2:])
    return jnp.moveaxis(a, (1, 3), (0, 2))


def _from_chunks(a):
    nc, b, h, l, d = a.shape
    return jnp.moveaxis(a, (0, 2), (1, 3)).reshape(b, nc * l, h, d)


def mlstm_chunkwise(q, k, v, i_pre, f_pre):
    bsz, _, nh, dk = q.shape
    dv = v.shape[-1]
    q, k, v = _to_chunks(q * dk ** -0.5), _to_chunks(k), _to_chunks(v)
    log_i = _to_chunks(i_pre)
    b = jnp.cumsum(_to_chunks(jax.nn.log_sigmoid(f_pre)), axis=-1)
    causal = jnp.tril(jnp.ones((CHUNK, CHUNK), dtype=bool))

    def step(carry, inp):
        c, n, m = carry
        qc, kc, vc, ic, bc = inp
        log_d = jnp.where(causal, bc[..., :, None] - bc[..., None, :] + ic[..., None, :], -jnp.inf)
        log_inter = bc + m[..., None]
        m_t = jnp.maximum(log_inter, jnp.max(log_d, axis=-1))
        s = jnp.einsum('bhld,bhsd->bhls', qc, kc) * jnp.exp(log_d - m_t[..., None])
        inter = jnp.exp(log_inter - m_t)
        num = jnp.einsum('bhls,bhse->bhle', s, vc) + inter[..., None] * jnp.einsum('bhld,bhde->bhle', qc, c)
        den = jnp.sum(s, axis=-1) + inter * jnp.einsum('bhld,bhd->bhl', qc, n)
        h = num / jnp.maximum(jnp.abs(den), jnp.exp(-m_t))[..., None]
        b_last = bc[..., -1]
        log_w = b_last[..., None] - bc + ic
        m_new = jnp.maximum(b_last + m, jnp.max(log_w, axis=-1))
        w = jnp.exp(log_w - m_new[..., None])
        decay = jnp.exp(b_last + m - m_new)
        kw = kc * w[..., None]
        c = decay[..., None, None] * c + jnp.einsum('bhsd,bhse->bhde', kw, vc)
        n = decay[..., None] * n + jnp.sum(kw, axis=-2)
        return (c, n, m_new), h

    init = (jnp.zeros((bsz, nh, dk, dv), q.dtype), jnp.zeros((bsz, nh, dk), q.dtype),
            jnp.zeros((bsz, nh), q.dtype))
    _, h = lax.scan(step, init, (q, k, v, log_i, b))
    return _from_chunks(h)


def gated_delta_chunkwise(q, k, v, beta, log_alpha):
    bsz, _, nh, dk = q.shape
    dv = v.shape[-1]
    q, k, v = _to_chunks(q * dk ** -0.5), _to_chunks(k), _to_chunks(v)
    beta = _to_chunks(beta)
    g = jnp.cumsum(_to_chunks(log_alpha), axis=-1)
    incl = jnp.tril(jnp.ones((CHUNK, CHUNK), dtype=bool))
    strict = jnp.tril(jnp.ones((CHUNK, CHUNK), dtype=bool), -1)
    diff = g[..., :, None] - g[..., None, :]
    decay_strict = jnp.exp(jnp.where(strict, diff, -jnp.inf))
    decay_incl = jnp.exp(jnp.where(incl, diff, -jnp.inf))
    eg = jnp.exp(g)
    a = jnp.eye(CHUNK, dtype=q.dtype) + beta[..., :, None] * jnp.einsum('...ld,...sd->...ls', k, k) * decay_strict
    rhs = jnp.concatenate([beta[..., None] * v, (beta * eg)[..., None] * k], axis=-1)
    sol = lax.linalg.triangular_solve(a, rhs, left_side=True, lower=True, unit_diagonal=True)
    u, wk = sol[..., :dv], sol[..., dv:]
    qk = jnp.einsum('...ld,...sd->...ls', q, k) * decay_incl
    qg = q * eg[..., None]
    kg = k * jnp.exp(g[..., -1:] - g)[..., None]
    g_last = jnp.exp(g[..., -1])

    def step(s, inp):
        u_c, wk_c, qk_c, qg_c, kg_c, gl_c = inp
        w = u_c - jnp.einsum('bhld,bhde->bhle', wk_c, s)
        o = jnp.einsum('bhld,bhde->bhle', qg_c, s) + jnp.einsum('bhls,bhse->bhle', qk_c, w)
        s = gl_c[..., None, None] * s + jnp.einsum('bhld,bhle->bhde', kg_c, w)
        return s, o

    s0 = jnp.zeros((bsz, nh, dk, dv), q.dtype)
    _, o = lax.scan(step, s0, (u, wk, qk, qg, kg, g_last))
    return _from_chunks(o)


def parallel_mixers(h, w_in, m_igate_b, m_fgate_b, m_norm_g, g_conv_w, g_a_log, g_dt_bias, g_norm_g,
                    c_dw_w, c_dw_b, c_ln_g, c_ln_b, s_conv_w, w_out):
    f32 = jnp.float32
    bsz, t, _ = h.shape
    heads = lambda a: a.reshape(bsz, t, -1, HEAD_DIM)
    p = jnp.einsum('btd,dp->btp', h, w_in)
    split_at = np.cumsum(PROJ_SECTIONS)[:-1].tolist()
    (mq, mk, mv, mo, mi, mf, gq, gk, gv, gz, gb, ga, ca, cg, sb, sc, sx) = jnp.split(p, split_at, axis=-1)

    hm = mlstm_chunkwise(heads(mq).astype(f32), heads(mk).astype(f32), heads(mv).astype(f32),
                         (mi + m_igate_b).astype(f32), (mf + m_fgate_b).astype(f32))
    hm = rmsnorm(hm, m_norm_g.reshape(M_HEADS, HEAD_DIM)) * jax.nn.sigmoid(heads(mo).astype(f32))
    y_m = hm.reshape(bsz, t, GROUP_W).astype(h.dtype)

    qkv = jax.nn.silu(causal_dwconv(jnp.concatenate([gq, gk, gv], axis=-1), g_conv_w))
    gq, gk, gv = jnp.split(qkv, 3, axis=-1)
    beta = jax.nn.sigmoid(gb.astype(f32))
    log_alpha = -jnp.exp(g_a_log.astype(f32)) * jax.nn.softplus((ga + g_dt_bias).astype(f32))
    og = gated_delta_chunkwise(l2norm(heads(gq).astype(f32)), l2norm(heads(gk).astype(f32)),
                               heads(gv).astype(f32), beta, log_alpha)
    og = rmsnorm(og, g_norm_g) * jax.nn.silu(heads(gz).astype(f32))
    y_g = og.reshape(bsz, t, GROUP_W).astype(h.dtype)

    u = ca * jax.nn.sigmoid(cg)
    u = causal_dwconv(u, c_dw_w) + c_dw_b
    y_c = jax.nn.silu(layernorm(u, c_ln_g, c_ln_b))

    y_s = sb * causal_dwconv(sc * sx, s_conv_w)

    y = jnp.concatenate([y_m, y_g, y_c, y_s], axis=-1)
    return jnp.einsum('btc,cd->btd', y, w_out)


def swiglu(h, w_gate_up, w_down):
    gu = jnp.einsum('btd,df->btf', h, w_gate_up)
    gate, up = jnp.split(gu, 2, axis=-1)
    return jnp.einsum('btf,fd->btd', jax.nn.silu(gate) * up, w_down)


def setup_inputs(seed: int = 0) -> dict:
    key = jax.random.key(seed)
    ks = jax.random.split(key, 24)
    f32 = jnp.float32
    nrm = lambda k, shape, scale: jax.random.normal(k, shape, f32) * scale
    gain = lambda k, shape: 1.0 + 0.02 * jax.random.normal(k, shape, f32)
    dt = jnp.exp(jax.random.uniform(ks[8], (DEPTH, G_HEADS), f32, math.log(1e-3), math.log(1e-1)))
    return {
        'x': nrm(ks[0], (BATCH, SEQ, D_MODEL), 1.0),
        'norm1_g': gain(ks[1], (DEPTH, D_MODEL)),
        'w_in': nrm(ks[2], (DEPTH, D_MODEL, PROJ_W), D_MODEL ** -0.5),
        'm_igate_b': nrm(ks[3], (DEPTH, M_HEADS), 0.1),
        'm_fgate_b': jnp.linspace(3.0, 6.0, M_HEADS, dtype=f32)[None, :] + nrm(ks[4], (DEPTH, M_HEADS), 0.1),
        'm_norm_g': gain(ks[5], (DEPTH, GROUP_W)),
        'g_conv_w': nrm(ks[6], (DEPTH, GDN_CONV, 3 * GROUP_W), GDN_CONV ** -0.5),
        'g_a_log': jnp.log(jax.random.uniform(ks[7], (DEPTH, G_HEADS), f32, 1.0, 16.0)),
        'g_dt_bias': dt + jnp.log(-jnp.expm1(-dt)),
        'g_norm_g': gain(ks[9], (DEPTH, HEAD_DIM)),
        'c_dw_w': nrm(ks[10], (DEPTH, CONF_CONV, GROUP_W), CONF_CONV ** -0.5),
        'c_dw_b': nrm(ks[11], (DEPTH, GROUP_W), 0.01),
        'c_ln_g': gain(ks[12], (DEPTH, GROUP_W)),
        'c_ln_b': nrm(ks[13], (DEPTH, GROUP_W), 0.01),
        's_conv_w': nrm(ks[14], (DEPTH, SC_CONV, GROUP_W), SC_CONV ** -0.5),
        'w_out': nrm(ks[15], (DEPTH, D_MODEL, D_MODEL), D_MODEL ** -0.5),
        'norm2_g': gain(ks[16], (DEPTH, D_MODEL)),
        'w_gate_up': nrm(ks[17], (DEPTH, D_MODEL, 2 * D_FF), D_MODEL ** -0.5),
        'w_down': nrm(ks[18], (DEPTH, D_FF, D_MODEL), D_FF ** -0.5),
        'final_norm_g': gain(ks[19], (D_MODEL,)),
    }


def reference(x, norm1_g, w_in, m_igate_b, m_fgate_b, m_norm_g, g_conv_w, g_a_log, g_dt_bias, g_norm_g,
              c_dw_w, c_dw_b, c_ln_g, c_ln_b, s_conv_w, w_out, norm2_g, w_gate_up, w_down, final_norm_g):
    for l in range(DEPTH):
        x = x + parallel_mixers(rmsnorm(x, norm1_g[l]), w_in[l], m_igate_b[l], m_fgate_b[l], m_norm_g[l],
                                g_conv_w[l], g_a_log[l], g_dt_bias[l], g_norm_g[l], c_dw_w[l], c_dw_b[l],
                                c_ln_g[l], c_ln_b[l], s_conv_w[l], w_out[l])
        x = x + swiglu(rmsnorm(x, norm2_g[l]), w_gate_up[l], w_down[l])
    return rmsnorm(x, final_norm_g)
```

```python
import functools

import numpy as np
import jax
import jax.numpy as jnp
from jax import lax
from jax.experimental import pallas as pl
from jax.experimental.pallas import tpu as pltpu

D_MODEL = 1024
GROUP_W = 256
HEAD_DIM = 64
N_HEADS = 4
CHUNK = 64
GDN_CONV = 4
CONF_CONV = 31
SC_CONV = 3
D_FF = 2816
EPS = 1e-6

LANES = 128
N_SECT = 13
GATE_OFF = N_SECT * GROUP_W
PROJ_PAD = GATE_OFF + LANES
GATE_REP = 16
HALO = 32
N_CB = 8

S_MQ, S_MK, S_MV, S_MO, S_GQ, S_GK, S_GV, S_GZ, S_CA, S_CG, S_SB, S_SC, S_SX = range(13)

TT = 256
TM = 512
FF_CHUNKS = ((0, 1024), (1024, 2048), (2048, 2816))
VMEM_LIMIT = 60 * 1024 * 1024

f32 = jnp.float32
bf16 = jnp.bfloat16


def _dot(a, b):
    return jnp.dot(a.astype(bf16), b.astype(bf16), preferred_element_type=f32)


def _dot_nt(a, b):
    return lax.dot_general(a.astype(bf16), b.astype(bf16), (((1,), (1,)), ((), ())), preferred_element_type=f32)


def _split2(x):
    hi = x.astype(bf16)
    lo = (x - hi.astype(f32)).astype(bf16)
    return hi, lo


def _split3(x):
    hi = x.astype(bf16)
    r = x - hi.astype(f32)
    mid = r.astype(bf16)
    lo = (r - mid.astype(f32)).astype(bf16)
    return hi, mid, lo


def _dot3(a, b):
    ah, al = _split2(a)
    bh, bl = _split2(b)
    d = functools.partial(jnp.dot, preferred_element_type=f32)
    return d(ah, bh) + (d(ah, bl) + d(al, bh))


def _softplus(x):
    return jnp.maximum(x, 0.0) + jnp.log1p(jnp.exp(-jnp.abs(x)))


def _silu(x):
    return x * jax.nn.sigmoid(x)


def _bd(y, lo_mask):
    return jnp.concatenate([jnp.where(lo_mask, y, 0.0), jnp.where(lo_mask, 0.0, y)], axis=0)


def _fold(z, lo_mask):
    return jnp.where(lo_mask, z[:CHUNK], z[CHUNK:])


def _cummax_rows(x, row):
    k = 1
    while k < CHUNK:
        x = jnp.maximum(x, jnp.where(row >= k, pltpu.roll(x, k, 0), -jnp.inf))
        k *= 2
    return x


def _group_mean_sq(x, ones_blk):
    hi, lo = _split2(x * x)
    d = functools.partial(jnp.dot, preferred_element_type=f32)
    return (d(hi, ones_blk) + d(lo, ones_blk)) * (1.0 / HEAD_DIM)


def _mixer_kernel(x_ref, n1g_ref, win_ref, gvec_ref, sel_ref, mng_ref, gcw_ref, gng_ref, cdw_ref, cvec_ref,
                  scw_ref, wout_ref, o_ref,
                  p_ref, cb_ref, gqkv_ref, ubuf_ref, y_ref, hm_ref, hg_ref, mc_ref, mm_ref, gs_ref):
    t = pl.program_id(1)

    @pl.when(t == 0)
    def _():
        p_ref[0:HALO, :] = jnp.zeros((HALO, PROJ_PAD), f32)
        mc_ref[...] = jnp.zeros_like(mc_ref)
        mm_ref[...] = jnp.zeros_like(mm_ref)
        gs_ref[...] = jnp.zeros_like(gs_ref)

    x = x_ref[0]
    ms = jnp.mean(x * x, axis=-1, keepdims=True)
    h = (x * lax.rsqrt(ms + EPS) * n1g_ref[...]).astype(bf16)
    p_ref[HALO:, :] = jnp.dot(h, win_ref[...], preferred_element_type=f32)

    def sect(s, rows=slice(HALO, HALO + TT), lo=0, width=GROUP_W):
        return p_ref[rows, s * GROUP_W + lo: s * GROUP_W + lo + width]

    lane1 = lax.broadcasted_iota(jnp.int32, (1, LANES), 1)
    gsel = lane1 % GATE_REP
    g_pre = p_ref[HALO:, GATE_OFF:] + gvec_ref[0:1, :]
    a_coef = -jnp.exp(gvec_ref[1:2, :])
    gval = jnp.where(gsel < 4, g_pre,
                     jnp.where(gsel < 8, -_softplus(-g_pre),
                               jnp.where(gsel < 12, jax.nn.sigmoid(g_pre), a_coef * _softplus(g_pre))))
    g_hi, g_mid, g_lo = _split3(gval)
    pieces = jnp.where(lane1 < GATE_REP, g_hi, jnp.where(lane1 < 2 * GATE_REP, g_mid, g_lo))
    cb_ref[...] = jnp.dot(pieces, sel_ref[...], preferred_element_type=f32)

    lane256 = lax.broadcasted_iota(jnp.int32, (GROUP_W, GROUP_W), 1) // HEAD_DIM
    row256 = lax.broadcasted_iota(jnp.int32, (GROUP_W, GROUP_W), 0) // HEAD_DIM
    ones4 = (lane256 == row256).astype(bf16)
    for j, s in enumerate((S_GQ, S_GK, S_GV)):
        acc = None
        for k in range(GDN_CONV):
            rows = slice(HALO - (GDN_CONV - 1) + k, HALO - (GDN_CONV - 1) + k + TT)
            term = sect(s, rows) * gcw_ref[k:k + 1, j * GROUP_W:(j + 1) * GROUP_W]
            acc = term if acc is None else acc + term
        v = _silu(acc)
        if s != S_GV:
            v = v * lax.rsqrt(_group_mean_sq(v, ones4) * HEAD_DIM + EPS)
            if s == S_GQ:
                v = v * (HEAD_DIM ** -0.5)
        gqkv_ref[:, j * GROUP_W:(j + 1) * GROUP_W] = v

    row = lax.broadcasted_iota(jnp.int32, (CHUNK, LANES), 0)
    lane = lax.broadcasted_iota(jnp.int32, (CHUNK, LANES), 1)
    s_idx = lane % CHUNK
    lo1 = lane < CHUNK
    lo2 = (lax.broadcasted_iota(jnp.int32, (CHUNK, 2 * LANES), 1) % LANES) < CHUNK
    incl = s_idx <= row
    strict = s_idx < row
    u_mask = (row > s_idx).astype(f32)
    eye = (row == s_idx).astype(f32)
    r3 = lax.broadcasted_iota(jnp.int32, (CHUNK, 3 * CHUNK), 0)
    c3 = lax.broadcasted_iota(jnp.int32, (CHUNK, 3 * CHUNK), 1) % CHUNK
    tri3 = (c3 <= r3).astype(bf16)
    ones_blk = jnp.ones((CHUNK, LANES), f32)

    def prefix(w_ext):
        hi, mid, lo = _split3(w_ext)
        return jnp.dot(tri3, jnp.concatenate([hi, mid, lo], axis=0), preferred_element_type=f32)

    def chunk_body(c, carry):
        r0 = pl.multiple_of(c * CHUNK, CHUNK)
        rows = pl.ds(r0, CHUNK)
        prow = pl.ds(r0 + HALO, CHUNK)
        for p in range(2):
            ln = slice(p * LANES, (p + 1) * LANES)
            q = p_ref[prow, S_MQ * GROUP_W + p * LANES: S_MQ * GROUP_W + (p + 1) * LANES]
            k = p_ref[prow, S_MK * GROUP_W + p * LANES: S_MK * GROUP_W + (p + 1) * LANES]
            v = p_ref[prow, S_MV * GROUP_W + p * LANES: S_MV * GROUP_W + (p + 1) * LANES]
            cb_i = cb_ref[rows, (2 * p) * LANES:(2 * p + 1) * LANES]
            cb_lf = cb_ref[rows, (2 * p + 1) * LANES:(2 * p + 2) * LANES]
            pre = prefix(jnp.concatenate([cb_lf * u_mask + cb_i * eye, cb_lf], axis=1))
            log_d, cb_b = pre[:, :LANES], pre[:, LANES:]
            m_prev = mm_ref[p, 0:1, :]
            mmax = jnp.maximum(m_prev, _cummax_rows(cb_i - cb_b, row))
            m_t = cb_b + mmax
            d = jnp.exp(jnp.where(incl, log_d - m_t, -jnp.inf))
            s = _dot_nt(q, _bd(k, lo1)) * d
            inter = jnp.exp(m_prev - mmax)
            v_ext = jnp.concatenate([v, ones_blk], axis=1)
            c_ext = mc_ref[p]
            res = _dot(jnp.concatenate([s, inter * q], axis=1),
                       jnp.concatenate([_bd(v_ext, lo2), _bd(c_ext, lo2)], axis=0))
            num, den = res[:, :LANES], res[:, LANES:]
            hm_ref[rows, ln] = num / jnp.maximum(jnp.abs(den), jnp.exp(-m_t))
            m_new = m_t[CHUNK - 1:CHUNK, :]
            b_last = cb_b[CHUNK - 1:CHUNK, :]
            w = jnp.exp(b_last - cb_b + cb_i - m_new)
            decay = jnp.exp(b_last + m_prev - m_new)
            upd = _dot((k * w).T, v_ext)
            mc_ref[p] = jnp.concatenate([decay, decay], axis=1) * c_ext + _fold(upd, lo2)
            mm_ref[p] = jnp.broadcast_to(m_new, (8, LANES))

            gq = gqkv_ref[rows, p * LANES:(p + 1) * LANES]
            gk = gqkv_ref[rows, GROUP_W + p * LANES: GROUP_W + (p + 1) * LANES]
            gv = gqkv_ref[rows, 2 * GROUP_W + p * LANES: 2 * GROUP_W + (p + 1) * LANES]
            cb_beta = cb_ref[rows, (4 + 2 * p) * LANES:(5 + 2 * p) * LANES]
            cb_la = cb_ref[rows, (5 + 2 * p) * LANES:(6 + 2 * p) * LANES]
            pre = prefix(jnp.concatenate([cb_la * u_mask, cb_la], axis=1))
            diff, cb_g = pre[:, :LANES], pre[:, LANES:]
            dec = jnp.exp(jnp.where(incl, diff, -jnp.inf))
            kq = _dot_nt(jnp.concatenate([gk, gq], axis=0), _bd(gk, lo1))
            a = cb_beta * kq[:CHUNK] * jnp.where(strict, dec, 0.0)
            qkd = kq[CHUNK:] * dec
            tn = eye - a
            pw = _dot3(a, _bd(a, lo1))
            for it in range(5):
                tn = tn + _dot3(tn, _bd(pw, lo1))
                if it < 4:
                    pw = _dot3(pw, _bd(pw, lo1))
            eg = jnp.exp(cb_g)
            rhs = jnp.concatenate([cb_beta * gv, cb_beta * eg * gk], axis=1)
            sol = _dot3(tn, _bd(rhs, lo2))
            u_c, wk_c = sol[:, :LANES], sol[:, LANES:]
            g_last = cb_g[CHUNK - 1:CHUNK, :]
            kg = gk * jnp.exp(g_last - cb_g)
            st = gs_ref[p]
            st_bd = _bd(st, lo1)
            w_c = u_c - _dot(wk_c, st_bd)
            hg_ref[rows, ln] = _dot(jnp.concatenate([gq * eg, qkd], axis=1),
                                    jnp.concatenate([st_bd, _bd(w_c, lo1)], axis=0))
            gs_ref[p] = jnp.exp(g_last) * st + _fold(_dot(kg.T, w_c), lo1)
        return carry

    lax.fori_loop(0, TT // CHUNK, chunk_body, 0)

    hm = hm_ref[...]
    hm = hm * lax.rsqrt(_group_mean_sq(hm, ones4) + EPS) * mng_ref[...]
    y_ref[:, 0:GROUP_W] = (hm * jax.nn.sigmoid(sect(S_MO))).astype(bf16)
    hg = hg_ref[...]
    hg = hg * lax.rsqrt(_group_mean_sq(hg, ones4) + EPS) * gng_ref[...]
    y_ref[:, GROUP_W:2 * GROUP_W] = (hg * _silu(sect(S_GZ))).astype(bf16)

    allrows = slice(0, HALO + TT)
    ubuf_ref[...] = sect(S_CA, allrows) * jax.nn.sigmoid(sect(S_CG, allrows))
    acc = jnp.broadcast_to(cvec_ref[0:1, :], (TT, GROUP_W))
    for k in range(CONF_CONV):
        off = HALO - (CONF_CONV - 1) + k
        acc = acc + ubuf_ref[off:off + TT, :] * cdw_ref[k:k + 1, :]
    mu = jnp.mean(acc, axis=-1, keepdims=True)
    xc = acc - mu
    var = jnp.mean(xc * xc, axis=-1, keepdims=True)
    yc = xc * lax.rsqrt(var + EPS) * cvec_ref[1:2, :] + cvec_ref[2:3, :]
    y_ref[:, 2 * GROUP_W:3 * GROUP_W] = _silu(yc).astype(bf16)

    acc = None
    for k in range(SC_CONV):
        rows = slice(HALO - (SC_CONV - 1) + k, HALO - (SC_CONV - 1) + k + TT)
        term = sect(S_SC, rows) * sect(S_SX, rows) * scw_ref[k:k + 1, :]
        acc = term if acc is None else acc + term
    y_ref[:, 3 * GROUP_W:] = (sect(S_SB) * acc).astype(bf16)

    o_ref[0] = x + jnp.dot(y_ref[...], wout_ref[...], preferred_element_type=f32)

    p_ref[0:HALO, S_GQ * GROUP_W:GATE_OFF] = p_ref[TT:TT + HALO, S_GQ * GROUP_W:GATE_OFF]


def _const_spec(shape):
    return pl.BlockSpec(shape, lambda *_: (0,) * len(shape))


def _mixer_call(x, n1g, win, gvec, sel, mng, gcw, gng, cdw, cvec, scw, wout):
    bsz, seq, _ = x.shape
    grid = (bsz, seq // TT)
    xspec = pl.BlockSpec((1, TT, D_MODEL), lambda b, t: (b, t, 0))
    args = (n1g, win, gvec, sel, mng, gcw, gng, cdw, cvec, scw, wout)
    return pl.pallas_call(
        _mixer_kernel,
        grid=grid,
        in_specs=[xspec] + [_const_spec(a.shape) for a in args],
        out_specs=xspec,
        out_shape=jax.ShapeDtypeStruct(x.shape, f32),
        scratch_shapes=[
            pltpu.VMEM((HALO + TT, PROJ_PAD), f32),
            pltpu.VMEM((TT, N_CB * LANES), f32),
            pltpu.VMEM((TT, 3 * GROUP_W), f32),
            pltpu.VMEM((HALO + TT, GROUP_W), f32),
            pltpu.VMEM((TT, D_MODEL), bf16),
            pltpu.VMEM((TT, GROUP_W), f32),
            pltpu.VMEM((TT, GROUP_W), f32),
            pltpu.VMEM((2, CHUNK, 2 * LANES), f32),
            pltpu.VMEM((2, 8, LANES), f32),
            pltpu.VMEM((2, CHUNK, LANES), f32),
        ],
        compiler_params=pltpu.CompilerParams(dimension_semantics=("arbitrary", "arbitrary"),
                                             vmem_limit_bytes=VMEM_LIMIT),
        name="mixer",
    )(x, *args)


def _ffn_kernel(x_ref, g_ref, wgu_ref, wd_ref, fg_ref, o_ref, *, final):
    x = x_ref[...]
    ms = jnp.mean(x * x, axis=-1, keepdims=True)
    h = (x * lax.rsqrt(ms + EPS) * g_ref[...]).astype(bf16)
    acc = x
    for a, b in FF_CHUNKS:
        gate = jnp.dot(h, wgu_ref[:, a:b], preferred_element_type=f32)
        up = jnp.dot(h, wgu_ref[:, D_FF + a:D_FF + b], preferred_element_type=f32)
        act = (_silu(gate) * up).astype(bf16)
        acc = acc + jnp.dot(act, wd_ref[a:b, :], preferred_element_type=f32)
    if final:
        ms = jnp.mean(acc * acc, axis=-1, keepdims=True)
        acc = acc * lax.rsqrt(ms + EPS) * fg_ref[...]
    o_ref[...] = acc


def _ffn_call(x2d, g, wgu, wd, fg, final):
    n = x2d.shape[0]
    xspec = pl.BlockSpec((TM, D_MODEL), lambda i: (i, 0))
    return pl.pallas_call(
        functools.partial(_ffn_kernel, final=final),
        grid=(n // TM,),
        in_specs=[xspec, _const_spec(g.shape), _const_spec(wgu.shape), _const_spec(wd.shape), _const_spec(fg.shape)],
        out_specs=xspec,
        out_shape=jax.ShapeDtypeStruct(x2d.shape, f32),
        compiler_params=pltpu.CompilerParams(dimension_semantics=("arbitrary",), vmem_limit_bytes=VMEM_LIMIT),
        name="ffn_final" if final else "ffn",
    )(x2d, g, wgu, wd, fg)


def _sel_matrix():
    sel = np.zeros((LANES, N_CB * LANES), np.float32)
    for blk, (kind, p) in enumerate([(0, 0), (1, 0), (0, 1), (1, 1), (2, 0), (3, 0), (2, 1), (3, 1)]):
        for half in range(2):
            src = 4 * kind + 2 * p + half
            for rep in range(3):
                sel[rep * GATE_REP + src, blk * LANES + half * CHUNK: blk * LANES + (half + 1) * CHUNK] = 1.0
    return jnp.asarray(sel, bf16)


def _pad_rows(a, rows):
    return jnp.pad(a, ((0, rows - a.shape[0]), (0, 0)))


def _prep_w_in(w):
    o = np.cumsum([0, 256, 256, 256, 256, 4, 4, 256, 256, 256, 256, 4, 4, 256, 256, 256, 256, 256])
    col = lambda i: w[:, o[i]:o[i + 1]]
    gates = jnp.concatenate([col(4), col(5), col(10), col(11)], axis=1)
    gate_blk = jnp.concatenate([gates, gates, gates, jnp.zeros((D_MODEL, LANES - 3 * GATE_REP), w.dtype)], axis=1)
    sections = [col(0) * (HEAD_DIM ** -0.5), col(1), col(2), col(3), col(6), col(7), col(8), col(9),
                col(12), col(13), col(14), col(15), col(16)]
    return jnp.concatenate(sections + [gate_blk], axis=1).astype(bf16)


def _gate_vec(igate_b, fgate_b, dt_bias, a_log):
    z4 = jnp.zeros((N_HEADS,), f32)
    bias = jnp.concatenate([igate_b, fgate_b, z4, dt_bias])
    alog = jnp.concatenate([z4, z4, z4, a_log])
    pad = jnp.zeros((LANES - 3 * GATE_REP,), f32)
    rows = [jnp.concatenate([v, v, v, pad]) for v in (bias, alog)]
    return _pad_rows(jnp.stack(rows), 8)


def kernel(x, norm1_g, w_in, m_igate_b, m_fgate_b, m_norm_g, g_conv_w, g_a_log, g_dt_bias, g_norm_g,
           c_dw_w, c_dw_b, c_ln_g, c_ln_b, s_conv_w, w_out, norm2_g, w_gate_up, w_down, final_norm_g):
    bsz, seq, d = x.shape
    depth = w_in.shape[0]
    sel = _sel_matrix()
    fg = final_norm_g.reshape(1, d)
    for l in range(depth):
        x = _mixer_call(
            x, norm1_g[l].reshape(1, d), _prep_w_in(w_in[l]),
            _gate_vec(m_igate_b[l], m_fgate_b[l], g_dt_bias[l], g_a_log[l]), sel,
            m_norm_g[l].reshape(1, GROUP_W), _pad_rows(g_conv_w[l], 8),
            jnp.tile(g_norm_g[l], N_HEADS).reshape(1, GROUP_W), _pad_rows(c_dw_w[l], 32),
            _pad_rows(jnp.stack([c_dw_b[l], c_ln_g[l], c_ln_b[l]]), 8), _pad_rows(s_conv_w[l], 8),
            w_out[l].astype(bf16))
        x = _ffn_call(x.reshape(bsz * seq, d), norm2_g[l].reshape(1, d), w_gate_up[l].astype(bf16),
                      w_down[l].astype(bf16), fg, final=(l == depth - 1)).reshape(bsz, seq, d)
    return x
```

```python
import functools

import numpy as np
import jax
import jax.numpy as jnp
from jax import lax
from jax.experimental import pallas as pl
from jax.experimental.pallas import tpu as pltpu

D_MODEL = 1024
GROUP_W = 256
HEAD_DIM = 64
N_HEADS = 4
CHUNK = 64
GDN_CONV = 4
CONF_CONV = 31
SC_CONV = 3
D_FF = 2816
EPS = 1e-6

LANES = 128
SUBLANES = 8
N_SECT = 13
GATE_OFF = N_SECT * GROUP_W
PROJ_PAD = GATE_OFF + LANES
GATE_REP = 16
N_CB = 8

S_GQ, S_GK, S_GV, S_MQ, S_MK, S_MV, S_MO, S_GZ, S_CA, S_CG, S_SB, S_SC, S_SX = range(13)
P_FIRST = S_MQ

NB = 4
ROWS = NB * CHUNK
CONF_HALO = 32
TM = 512
FF_CHUNKS = ((0, 1024), (1024, 2048), (2048, 2816))
VMEM_LIMIT = 60 * 1024 * 1024

f32 = jnp.float32
bf16 = jnp.bfloat16


def _dot(a, b):
    return jnp.dot(a.astype(bf16), b.astype(bf16), preferred_element_type=f32)


def _dot_nt(a, b):
    return lax.dot_general(a.astype(bf16), b.astype(bf16), (((1,), (1,)), ((), ())), preferred_element_type=f32)


def _split2(x):
    hi = x.astype(bf16)
    lo = (x - hi.astype(f32)).astype(bf16)
    return hi, lo


def _split3(x):
    hi = x.astype(bf16)
    r = x - hi.astype(f32)
    mid = r.astype(bf16)
    lo = (r - mid.astype(f32)).astype(bf16)
    return hi, mid, lo


def _softplus(x):
    return jnp.maximum(x, 0.0) + jnp.log1p(jnp.exp(-jnp.abs(x)))


def _silu(x):
    return x * jax.nn.sigmoid(x)


def _bd(y, lo_mask):
    return jnp.concatenate([jnp.where(lo_mask, y, 0.0), jnp.where(lo_mask, 0.0, y)], axis=0)


def _fold(z, lo_mask):
    return jnp.where(lo_mask, z[:CHUNK], z[CHUNK:])


def _cummax_rows(x, row):
    k = 1
    while k < CHUNK:
        x = jnp.maximum(x, jnp.where(row >= k, pltpu.roll(x, k, 0), -jnp.inf))
        k *= 2
    return x


def _group_mean_sq(x, ones_blk):
    hi, lo = _split2(x * x)
    d = functools.partial(jnp.dot, preferred_element_type=f32)
    return (d(hi, ones_blk) + d(lo, ones_blk)) * (1.0 / HEAD_DIM)


def _causal_taps(buf_ref, b, w_ref, n_taps, halo, lanes):
    acc = None
    for k in range(n_taps):
        off = halo - (n_taps - 1) + k
        term = buf_ref[b, off:off + CHUNK, lanes] * w_ref[k:k + 1, lanes]
        acc = term if acc is None else acc + term
    return acc


def _mixer_kernel(x_ref, n1g_ref, win_ref, gvec_ref, sel_ref, mng_ref, gcw_ref, gng_ref, cdw_ref, cvec_ref,
                  scw_ref, wout_ref, o_ref,
                  p_ref, cb_ref, gbuf_ref, gqkv_ref, ubuf_ref, cxbuf_ref, y_ref, hm_ref, hg_ref,
                  mc_ref, mm_ref, gs_ref):
    t = pl.program_id(1)

    @pl.when(t == 0)
    def _():
        gbuf_ref[:, 0:SUBLANES, :] = jnp.zeros((NB, SUBLANES, 3 * GROUP_W), f32)
        ubuf_ref[:, 0:CONF_HALO, :] = jnp.zeros((NB, CONF_HALO, GROUP_W), f32)
        cxbuf_ref[:, 0:SUBLANES, :] = jnp.zeros((NB, SUBLANES, GROUP_W), f32)
        mc_ref[...] = jnp.zeros_like(mc_ref)
        mm_ref[...] = jnp.zeros_like(mm_ref)
        gs_ref[...] = jnp.zeros_like(gs_ref)

    x = x_ref[...].reshape(ROWS, D_MODEL)
    ms = jnp.mean(x * x, axis=-1, keepdims=True)
    h = (x * lax.rsqrt(ms + EPS) * n1g_ref[...]).astype(bf16)
    dotf = functools.partial(jnp.dot, preferred_element_type=f32)
    gbuf_ref[:, SUBLANES:, :] = dotf(h, win_ref[:, 0:3 * GROUP_W]).reshape(NB, CHUNK, 3 * GROUP_W)
    g_pre = dotf(h, win_ref[:, GATE_OFF:]) + gvec_ref[0:1, :]
    p_ref[:, 0:4 * GROUP_W] = dotf(h, win_ref[:, S_MQ * GROUP_W:S_GZ * GROUP_W])
    p_ref[:, 4 * GROUP_W:] = dotf(h, win_ref[:, S_GZ * GROUP_W:GATE_OFF])

    def sect(s, rows=slice(None)):
        return p_ref[rows, (s - P_FIRST) * GROUP_W:(s - P_FIRST + 1) * GROUP_W]

    lane1 = lax.broadcasted_iota(jnp.int32, (1, LANES), 1)
    gsel = lane1 % GATE_REP
    a_coef = -jnp.exp(gvec_ref[1:2, :])
    gval = jnp.where(gsel < 4, g_pre,
                     jnp.where(gsel < 8, -_softplus(-g_pre),
                               jnp.where(gsel < 12, jax.nn.sigmoid(g_pre), a_coef * _softplus(g_pre))))
    g_hi, g_mid, g_lo = _split3(gval)
    pieces = jnp.where(lane1 < GATE_REP, g_hi, jnp.where(lane1 < 2 * GATE_REP, g_mid, g_lo))
    cb_ref[...] = dotf(pieces, sel_ref[...])

    lane256 = lax.broadcasted_iota(jnp.int32, (GROUP_W, GROUP_W), 1) // HEAD_DIM
    row256 = lax.broadcasted_iota(jnp.int32, (GROUP_W, GROUP_W), 0) // HEAD_DIM
    ones4 = (lane256 == row256).astype(bf16)
    for b in range(NB):
        rows = slice(b * CHUNK, (b + 1) * CHUNK)
        for j in range(3):
            lanes = slice(j * GROUP_W, (j + 1) * GROUP_W)
            v = _silu(_causal_taps(gbuf_ref, b, gcw_ref, GDN_CONV, SUBLANES, lanes))
            if j < 2:
                v = v * lax.rsqrt(_group_mean_sq(v, ones4) * HEAD_DIM + EPS)
                if j == 0:
                    v = v * (HEAD_DIM ** -0.5)
            gqkv_ref[rows, lanes] = v
        gbuf_ref[b, 0:SUBLANES, :] = gbuf_ref[b, CHUNK:CHUNK + SUBLANES, :]

    row = lax.broadcasted_iota(jnp.int32, (CHUNK, LANES), 0)
    lane = lax.broadcasted_iota(jnp.int32, (CHUNK, LANES), 1)
    s_idx = lane % CHUNK
    lo1 = lane < CHUNK
    lo2 = (lax.broadcasted_iota(jnp.int32, (CHUNK, 2 * LANES), 1) % LANES) < CHUNK
    incl = s_idx <= row
    strict = s_idx < row
    u_mask = (row > s_idx).astype(f32)
    eye = (row == s_idx).astype(f32)
    same = lambda n: (row // n) == (s_idx // n)
    blk8, blk16, blk32 = same(8), same(16), same(32)
    r3 = lax.broadcasted_iota(jnp.int32, (CHUNK, 3 * CHUNK), 0)
    c3 = lax.broadcasted_iota(jnp.int32, (CHUNK, 3 * CHUNK), 1) % CHUNK
    tri3 = (c3 <= r3).astype(bf16)
    ones_blk = jnp.ones((CHUNK, LANES), f32)

    def prefix(w_ext):
        hi, mid, lo = _split3(w_ext)
        return dotf(tri3, jnp.concatenate([hi, mid, lo], axis=0))

    def pp(a, b_):
        return _dot(a, _bd(b_, lo1))

    def mlstm_instance(b, p):
        rows = slice(b * CHUNK, (b + 1) * CHUNK)
        ln = slice(p * LANES, (p + 1) * LANES)
        q, k, v = (p_ref[rows, (s - P_FIRST) * GROUP_W + p * LANES:(s - P_FIRST) * GROUP_W + (p + 1) * LANES]
                   for s in (S_MQ, S_MK, S_MV))
        cb_i = cb_ref[rows, (2 * p) * LANES:(2 * p + 1) * LANES]
        cb_lf = cb_ref[rows, (2 * p + 1) * LANES:(2 * p + 2) * LANES]
        pre = prefix(jnp.concatenate([cb_lf * u_mask + cb_i * eye, cb_lf], axis=1))
        qk = _dot_nt(q, _bd(k, lo1))
        yield
        log_d, cb_b = pre[:, :LANES], pre[:, LANES:]
        m_prev = mm_ref[b, p, 0:1, :]
        mmax = jnp.maximum(m_prev, _cummax_rows(cb_i - cb_b, row))
        m_t = cb_b + mmax
        s = qk * jnp.exp(jnp.where(incl, log_d - m_t, -jnp.inf))
        inter = jnp.exp(m_prev - mmax)
        v_ext = jnp.concatenate([v, ones_blk], axis=1)
        c_ext = mc_ref[b, p]
        res = _dot(jnp.concatenate([s, inter * q], axis=1),
                   jnp.concatenate([_bd(v_ext, lo2), _bd(c_ext, lo2)], axis=0))
        m_new = m_t[CHUNK - 1:CHUNK, :]
        b_last = cb_b[CHUNK - 1:CHUNK, :]
        w = jnp.exp(b_last - cb_b + cb_i - m_new)
        decay = jnp.exp(b_last + m_prev - m_new)
        upd = _dot((k * w).T, v_ext)
        yield
        num, den = res[:, :LANES], res[:, LANES:]
        hm_ref[rows, ln] = num / jnp.maximum(jnp.abs(den), jnp.exp(-m_t))
        mc_ref[b, p] = jnp.concatenate([decay, decay], axis=1) * c_ext + _fold(upd, lo2)
        mm_ref[b, p] = jnp.broadcast_to(m_new, (SUBLANES, LANES))

    def gdn_instance(b, p):
        rows = slice(b * CHUNK, (b + 1) * CHUNK)
        ln = slice(p * LANES, (p + 1) * LANES)
        gq = gqkv_ref[rows, p * LANES:(p + 1) * LANES]
        gk = gqkv_ref[rows, GROUP_W + p * LANES: GROUP_W + (p + 1) * LANES]
        gv = gqkv_ref[rows, 2 * GROUP_W + p * LANES: 2 * GROUP_W + (p + 1) * LANES]
        cb_beta = cb_ref[rows, (4 + 2 * p) * LANES:(5 + 2 * p) * LANES]
        cb_la = cb_ref[rows, (5 + 2 * p) * LANES:(6 + 2 * p) * LANES]
        pre = prefix(jnp.concatenate([cb_la * u_mask, cb_la], axis=1))
        kq = _dot_nt(jnp.concatenate([gk, gq], axis=0), _bd(gk, lo1))
        yield
        diff, cb_g = pre[:, :LANES], pre[:, LANES:]
        dec = jnp.exp(jnp.where(incl, diff, -jnp.inf))
        a = cb_beta * kq[:CHUNK] * jnp.where(strict, dec, 0.0)
        qkd = kq[CHUNK:] * dec
        a0 = jnp.where(blk8, a, 0.0)
        p1 = pp(a0, a0)
        yield
        tn = eye - a0
        tn = tn + pp(tn, p1)
        p2 = pp(p1, p1)
        yield
        tn = tn + pp(tn, p2)
        yield
        for inside, outside in ((blk16, blk8), (blk32, blk16), (None, blk32)):
            off = jnp.where(outside, 0.0, a) if inside is None else jnp.where(inside & ~outside, a, 0.0)
            m_off = pp(off, tn)
            yield
            tn = tn - pp(tn, m_off)
            yield
        eg = jnp.exp(cb_g)
        rhs = jnp.concatenate([cb_beta * gv, cb_beta * eg * gk], axis=1)
        sol = _dot(tn, _bd(rhs, lo2))
        yield
        u_c, wk_c = sol[:, :LANES], sol[:, LANES:]
        g_last = cb_g[CHUNK - 1:CHUNK, :]
        kg = gk * jnp.exp(g_last - cb_g)
        st = gs_ref[b, p]
        st_bd = _bd(st, lo1)
        w_c = u_c - _dot(wk_c, st_bd)
        yield
        hg_ref[rows, ln] = _dot(jnp.concatenate([gq * eg, qkd], axis=1),
                                jnp.concatenate([st_bd, _bd(w_c, lo1)], axis=0))
        gs_ref[b, p] = jnp.exp(g_last) * st + _fold(_dot(kg.T, w_c), lo1)

    live = [gdn_instance(b, p) for b in range(NB) for p in range(2)]
    live += [mlstm_instance(b, p) for b in range(NB) for p in range(2)]
    done = object()
    while live:
        live = [g for g in live if next(g, done) is not done]

    hm = hm_ref[...]
    hm = hm * lax.rsqrt(_group_mean_sq(hm, ones4) + EPS) * mng_ref[...]
    y_ref[:, 0:GROUP_W] = (hm * jax.nn.sigmoid(sect(S_MO))).astype(bf16)
    hg = hg_ref[...]
    hg = hg * lax.rsqrt(_group_mean_sq(hg, ones4) + EPS) * gng_ref[...]
    y_ref[:, GROUP_W:2 * GROUP_W] = (hg * _silu(sect(S_GZ))).astype(bf16)

    all256 = slice(0, GROUP_W)
    for b in range(NB):
        rows = slice(b * CHUNK, (b + 1) * CHUNK)
        ubuf_ref[b, CONF_HALO:, :] = sect(S_CA, rows) * jax.nn.sigmoid(sect(S_CG, rows))
        acc = jnp.broadcast_to(cvec_ref[0:1, :], (CHUNK, GROUP_W))
        first = CONF_HALO - (CONF_CONV - 1)
        for r in range(SUBLANES):
            nrows = CHUNK + (SUBLANES if r else 0)
            z = None
            for o in range(first, first + CONF_CONV):
                if o % SUBLANES != r:
                    continue
                term = ubuf_ref[b, o - r:o - r + nrows, :] * cdw_ref[o - first:o - first + 1, :]
                z = term if z is None else z + term
            acc = acc + (z if r == 0 else z[r:r + CHUNK])
        ubuf_ref[b, 0:CONF_HALO, :] = ubuf_ref[b, CHUNK:CHUNK + CONF_HALO, :]
        mu = jnp.mean(acc, axis=-1, keepdims=True)
        xc = acc - mu
        var = jnp.mean(xc * xc, axis=-1, keepdims=True)
        yc = xc * lax.rsqrt(var + EPS) * cvec_ref[1:2, :] + cvec_ref[2:3, :]
        y_ref[rows, 2 * GROUP_W:3 * GROUP_W] = _silu(yc).astype(bf16)

        cxbuf_ref[b, SUBLANES:, :] = sect(S_SC, rows) * sect(S_SX, rows)
        sconv = _causal_taps(cxbuf_ref, b, scw_ref, SC_CONV, SUBLANES, all256)
        cxbuf_ref[b, 0:SUBLANES, :] = cxbuf_ref[b, CHUNK:CHUNK + SUBLANES, :]
        y_ref[rows, 3 * GROUP_W:] = (sect(S_SB, rows) * sconv).astype(bf16)

    o_ref[...] = (x + dotf(y_ref[...], wout_ref[...])).reshape(NB, CHUNK, D_MODEL)


def _const_spec(shape):
    return pl.BlockSpec(shape, lambda *_: (0,) * len(shape))


def _mixer_call(x, n1g, win, gvec, sel, mng, gcw, gng, cdw, cvec, scw, wout):
    bsz, seq, _ = x.shape
    grid = (bsz // NB, seq // CHUNK)
    xspec = pl.BlockSpec((NB, CHUNK, D_MODEL), lambda b, t: (b, t, 0))
    args = (n1g, win, gvec, sel, mng, gcw, gng, cdw, cvec, scw, wout)
    return pl.pallas_call(
        _mixer_kernel,
        grid=grid,
        in_specs=[xspec] + [_const_spec(a.shape) for a in args],
        out_specs=xspec,
        out_shape=jax.ShapeDtypeStruct(x.shape, f32),
        scratch_shapes=[
            pltpu.VMEM((ROWS, (N_SECT - P_FIRST) * GROUP_W), f32),
            pltpu.VMEM((ROWS, N_CB * LANES), f32),
            pltpu.VMEM((NB, SUBLANES + CHUNK, 3 * GROUP_W), f32),
            pltpu.VMEM((ROWS, 3 * GROUP_W), f32),
            pltpu.VMEM((NB, CONF_HALO + CHUNK, GROUP_W), f32),
            pltpu.VMEM((NB, SUBLANES + CHUNK, GROUP_W), f32),
            pltpu.VMEM((ROWS, D_MODEL), bf16),
            pltpu.VMEM((ROWS, GROUP_W), f32),
            pltpu.VMEM((ROWS, GROUP_W), f32),
            pltpu.VMEM((NB, 2, CHUNK, 2 * LANES), f32),
            pltpu.VMEM((NB, 2, SUBLANES, LANES), f32),
            pltpu.VMEM((NB, 2, CHUNK, LANES), f32),
        ],
        compiler_params=pltpu.CompilerParams(dimension_semantics=("arbitrary", "arbitrary"),
                                             vmem_limit_bytes=VMEM_LIMIT),
        name="mixer",
    )(x, *args)


def _ffn_kernel(x_ref, g_ref, wgu_ref, wd_ref, fg_ref, o_ref, *, final):
    x = x_ref[...]
    ms = jnp.mean(x * x, axis=-1, keepdims=True)
    h = (x * lax.rsqrt(ms + EPS) * g_ref[...]).astype(bf16)
    acc = x
    for a, b in FF_CHUNKS:
        gate = jnp.dot(h, wgu_ref[:, a:b], preferred_element_type=f32)
        up = jnp.dot(h, wgu_ref[:, D_FF + a:D_FF + b], preferred_element_type=f32)
        act = (_silu(gate) * up).astype(bf16)
        acc = acc + jnp.dot(act, wd_ref[a:b, :], preferred_element_type=f32)
    if final:
        ms = jnp.mean(acc * acc, axis=-1, keepdims=True)
        acc = acc * lax.rsqrt(ms + EPS) * fg_ref[...]
    o_ref[...] = acc


def _ffn_call(x2d, g, wgu, wd, fg, final):
    n = x2d.shape[0]
    xspec = pl.BlockSpec((TM, D_MODEL), lambda i: (i, 0))
    return pl.pallas_call(
        functools.partial(_ffn_kernel, final=final),
        grid=(n // TM,),
        in_specs=[xspec, _const_spec(g.shape), _const_spec(wgu.shape), _const_spec(wd.shape), _const_spec(fg.shape)],
        out_specs=xspec,
        out_shape=jax.ShapeDtypeStruct(x2d.shape, f32),
        compiler_params=pltpu.CompilerParams(dimension_semantics=("arbitrary",), vmem_limit_bytes=VMEM_LIMIT),
        name="ffn_final" if final else "ffn",
    )(x2d, g, wgu, wd, fg)


def _sel_matrix():
    sel = np.zeros((LANES, N_CB * LANES), np.float32)
    for blk, (kind, p) in enumerate([(0, 0), (1, 0), (0, 1), (1, 1), (2, 0), (3, 0), (2, 1), (3, 1)]):
        for half in range(2):
            src = 4 * kind + 2 * p + half
            for rep in range(3):
                sel[rep * GATE_REP + src, blk * LANES + half * CHUNK: blk * LANES + (half + 1) * CHUNK] = 1.0
    return jnp.asarray(sel, bf16)


def _pad_rows(a, rows):
    return jnp.pad(a, ((0, rows - a.shape[0]), (0, 0)))


def _prep_w_in(w):
    o = np.cumsum([0, 256, 256, 256, 256, 4, 4, 256, 256, 256, 256, 4, 4, 256, 256, 256, 256, 256])
    col = lambda i: w[:, o[i]:o[i + 1]]
    gates = jnp.concatenate([col(4), col(5), col(10), col(11)], axis=1)
    gate_blk = jnp.concatenate([gates, gates, gates, jnp.zeros((D_MODEL, LANES - 3 * GATE_REP), w.dtype)], axis=1)
    sections = [col(6), col(7), col(8),
                col(0) * (HEAD_DIM ** -0.5), col(1), col(2), col(3),
                col(9), col(12), col(13), col(14), col(15), col(16)]
    return jnp.concatenate(sections + [gate_blk], axis=1).astype(bf16)


def _gate_vec(igate_b, fgate_b, dt_bias, a_log):
    z4 = jnp.zeros((N_HEADS,), f32)
    bias = jnp.concatenate([igate_b, fgate_b, z4, dt_bias])
    alog = jnp.concatenate([z4, z4, z4, a_log])
    pad = jnp.zeros((LANES - 3 * GATE_REP,), f32)
    rows = [jnp.concatenate([v, v, v, pad]) for v in (bias, alog)]
    return _pad_rows(jnp.stack(rows), SUBLANES)


def kernel(x, norm1_g, w_in, m_igate_b, m_fgate_b, m_norm_g, g_conv_w, g_a_log, g_dt_bias, g_norm_g,
           c_dw_w, c_dw_b, c_ln_g, c_ln_b, s_conv_w, w_out, norm2_g, w_gate_up, w_down, final_norm_g):
    bsz, seq, d = x.shape
    depth = w_in.shape[0]
    sel = _sel_matrix()
    fg = final_norm_g.reshape(1, d)
    for l in range(depth):
        x = _mixer_call(
            x, norm1_g[l].reshape(1, d), _prep_w_in(w_in[l]),
            _gate_vec(m_igate_b[l], m_fgate_b[l], g_dt_bias[l], g_a_log[l]), sel,
            m_norm_g[l].reshape(1, GROUP_W), _pad_rows(g_conv_w[l], SUBLANES),
            jnp.tile(g_norm_g[l], N_HEADS).reshape(1, GROUP_W), _pad_rows(c_dw_w[l], 32),
            _pad_rows(jnp.stack([c_dw_b[l], c_ln_g[l], c_ln_b[l]]), SUBLANES), _pad_rows(s_conv_w[l], SUBLANES),
            w_out[l].astype(bf16))
        x = _ffn_call(x.reshape(bsz * seq, d), norm2_g[l].reshape(1, d), w_gate_up[l].astype(bf16),
                      w_down[l].astype(bf16), fg, final=(l == depth - 1)).reshape(bsz, seq, d)
    return x
```

```python
import functools

import numpy as np
import jax
import jax.numpy as jnp
from jax import lax
from jax.experimental import pallas as pl
from jax.experimental.pallas import tpu as pltpu

D_MODEL = 1024
GROUP_W = 256
HEAD_DIM = 64
N_HEADS = 4
CHUNK = 64
GDN_CONV = 4
CONF_CONV = 31
SC_CONV = 3
D_FF = 2816
EPS = 1e-6

LANES = 128
SUBLANES = 8
N_SECT = 13
GATE_OFF = N_SECT * GROUP_W
PROJ_PAD = GATE_OFF + LANES
GATE_REP = 16
N_CB = 8

S_GQ, S_GK, S_GV, S_MQ, S_MK, S_MV, S_MO, S_GZ, S_CA, S_CG, S_SB, S_SC, S_SX = range(13)

NB = 4
ROWS = NB * CHUNK
PROJ_STEP = 2 * LANES
CONF_HALO = 32
PROJ_EARLY = 4
CONV_EARLY = 2
SPREAD_STAGES = 12
CONV_GROUP = 2
TM = 1024
FF_CHUNKS = ((0, 1024), (1024, 2048), (2048, 2816))
VMEM_LIMIT = 60 * 1024 * 1024

f32 = jnp.float32
bf16 = jnp.bfloat16


def _dot(a, b):
    return jnp.dot(a.astype(bf16), b.astype(bf16), preferred_element_type=f32)


def _dot_nt(a, b):
    return lax.dot_general(a.astype(bf16), b.astype(bf16), (((1,), (1,)), ((), ())), preferred_element_type=f32)


def _split3(x):
    hi = x.astype(bf16)
    r = x - hi.astype(f32)
    mid = r.astype(bf16)
    lo = (r - mid.astype(f32)).astype(bf16)
    return hi, mid, lo


def _softplus(x):
    return jnp.maximum(x, 0.0) + jnp.log1p(jnp.exp(-jnp.abs(x)))


def _silu(x):
    return x * jax.nn.sigmoid(x)


def _bd(y, lo_mask):
    return jnp.concatenate([jnp.where(lo_mask, y, 0.0), jnp.where(lo_mask, 0.0, y)], axis=0)


def _fold(z, lo_mask):
    return jnp.where(lo_mask, z[:CHUNK], z[CHUNK:])


def _cummax_rows(x, row):
    k = 1
    while k < CHUNK:
        x = jnp.maximum(x, jnp.where(row >= k, pltpu.roll(x, k, 0), -jnp.inf))
        k *= 2
    return x


def _group_mean_sq(x, ones_blk):
    return jnp.dot((x * x).astype(bf16), ones_blk, preferred_element_type=f32) * (1.0 / HEAD_DIM)


def _shifted_history_taps(cur, sh_ref, b, w_ref, n_taps, lanes):
    acc = cur[:, lanes] * w_ref[n_taps - 1:n_taps, lanes]
    for j in range(1, n_taps):
        acc = acc + sh_ref[j - 1, b, 0:CHUNK, lanes] * w_ref[n_taps - 1 - j:n_taps - j, lanes]
    return acc


def _store_shifted(cur, sh_ref, b, n_shift):
    for j in range(1, n_shift + 1):
        sh_ref[j - 1, b, j:CHUNK + j, :] = cur


def _roll_history(sh_ref, b, n_shift):
    for j in range(n_shift):
        sh_ref[j, b, 0:SUBLANES, :] = sh_ref[j, b, CHUNK:CHUNK + SUBLANES, :]


def _mixer_kernel(xc_ref, xn_ref, n1g_ref, win_ref, gvec_ref, sel_ref, mng_ref, gcw_ref, gng_ref, cdw_ref, cvec_ref,
                  scw_ref, wout_ref, o_ref,
                  cur, nxt, h_ref, cb_ref, gsh_ref, gqkv_ref, ush_ref, cxsh_ref, y_ref, hm_ref, hg_ref,
                  mc_ref, mm_ref, gs_ref):
    t = pl.program_id(1)
    dotf = functools.partial(jnp.dot, preferred_element_type=f32)

    def norm_rows(x_ref):
        x = x_ref[...].reshape(ROWS, D_MODEL)
        ms = jnp.mean(x * x, axis=-1, keepdims=True)
        h_ref[...] = (x * lax.rsqrt(ms + EPS) * n1g_ref[...]).astype(bf16)

    @pl.when(t == 0)
    def _():
        gsh_ref[...] = jnp.zeros_like(gsh_ref)
        ush_ref[...] = jnp.zeros_like(ush_ref)
        cxsh_ref[...] = jnp.zeros_like(cxsh_ref)
        mc_ref[...] = jnp.zeros_like(mc_ref)
        mm_ref[...] = jnp.zeros_like(mm_ref)
        gs_ref[...] = jnp.zeros_like(gs_ref)
        norm_rows(xc_ref)
        nxt[...] = dotf(h_ref[...], win_ref[...])

    cur[...] = nxt[...]

    norm_rows(xn_ref)

    def project_cols(c0):
        c1 = min(c0 + PROJ_STEP, PROJ_PAD)
        nxt[:, c0:c1] = dotf(h_ref[...], win_ref[:, c0:c1])

    pending = [functools.partial(project_cols, c0) for c0 in range(0, PROJ_PAD, PROJ_STEP)]

    def emit_projection(n=1):
        for _ in range(min(n, len(pending))):
            pending.pop(0)()

    def sect(s, rows=slice(None)):
        return cur[rows, s * GROUP_W:(s + 1) * GROUP_W]

    all256 = slice(0, GROUP_W)
    first = CONF_HALO - (CONF_CONV - 1)

    def conv_mixers():
        for b in range(NB):
            rows = slice(b * CHUNK, (b + 1) * CHUNK)
            u = sect(S_CA, rows) * jax.nn.sigmoid(sect(S_CG, rows))
            for r in range(SUBLANES):
                ush_ref[r, b, CONF_HALO - r:CONF_HALO - r + CHUNK, :] = u
            acc = jnp.broadcast_to(cvec_ref[0:1, :], (CHUNK, GROUP_W))
            for o in range(first, first + CONF_CONV):
                r = o % SUBLANES
                acc = acc + ush_ref[r, b, o - r:o - r + CHUNK, :] * cdw_ref[o - first:o - first + 1, :]
            for r in range(SUBLANES):
                ush_ref[r, b, 0:CONF_HALO, :] = ush_ref[r, b, CHUNK:CHUNK + CONF_HALO, :]
            mu = jnp.mean(acc, axis=-1, keepdims=True)
            xc = acc - mu
            var = jnp.mean(xc * xc, axis=-1, keepdims=True)
            yc = xc * lax.rsqrt(var + EPS) * cvec_ref[1:2, :] + cvec_ref[2:3, :]
            y_ref[rows, 2 * GROUP_W:3 * GROUP_W] = _silu(yc).astype(bf16)

            cx = sect(S_SC, rows) * sect(S_SX, rows)
            _store_shifted(cx, cxsh_ref, b, SC_CONV - 1)
            sconv = _shifted_history_taps(cx, cxsh_ref, b, scw_ref, SC_CONV, all256)
            _roll_history(cxsh_ref, b, SC_CONV - 1)
            y_ref[rows, 3 * GROUP_W:] = (sect(S_SB, rows) * sconv).astype(bf16)
            if (b + 1) % CONV_GROUP == 0:
                b0 = b + 1 - CONV_GROUP
                grows = slice(b0 * CHUNK, (b + 1) * CHUNK)
                x = xc_ref[b0:b + 1].reshape(CONV_GROUP * CHUNK, D_MODEL)
                part = dotf(y_ref[grows, 2 * GROUP_W:], wout_ref[2 * GROUP_W:, :])
                o_ref[b0:b + 1] = (x + part).reshape(CONV_GROUP, CHUNK, D_MODEL)
            yield

    conv = conv_mixers()
    for _ in range(CONV_EARLY):
        next(conv)
        emit_projection(2)

    lane1 = lax.broadcasted_iota(jnp.int32, (1, LANES), 1)
    gsel = lane1 % GATE_REP
    g_pre = cur[:, GATE_OFF:] + gvec_ref[0:1, :]
    a_coef = -jnp.exp(gvec_ref[1:2, :])
    gval = jnp.where(gsel < 4, g_pre,
                     jnp.where(gsel < 8, -_softplus(-g_pre),
                               jnp.where(gsel < 12, jax.nn.sigmoid(g_pre), a_coef * _softplus(g_pre))))
    g_hi, g_mid, g_lo = _split3(gval)
    pieces = jnp.where(lane1 < GATE_REP, g_hi, jnp.where(lane1 < 2 * GATE_REP, g_mid, g_lo))
    cb_ref[...] = dotf(pieces, sel_ref[...])
    emit_projection()

    lane256 = lax.broadcasted_iota(jnp.int32, (GROUP_W, GROUP_W), 1) // HEAD_DIM
    row256 = lax.broadcasted_iota(jnp.int32, (GROUP_W, GROUP_W), 0) // HEAD_DIM
    ones4 = (lane256 == row256).astype(bf16)
    for b in range(NB):
        rows = slice(b * CHUNK, (b + 1) * CHUNK)
        raw = cur[rows, 0:3 * GROUP_W]
        _store_shifted(raw, gsh_ref, b, GDN_CONV - 1)
        for j in range(3):
            lanes = slice(j * GROUP_W, (j + 1) * GROUP_W)
            v = _silu(_shifted_history_taps(raw, gsh_ref, b, gcw_ref, GDN_CONV, lanes))
            if j < 2:
                v = v * lax.rsqrt(_group_mean_sq(v, ones4) * HEAD_DIM + EPS)
                if j == 0:
                    v = v * (HEAD_DIM ** -0.5)
            gqkv_ref[rows, lanes] = v
        _roll_history(gsh_ref, b, GDN_CONV - 1)
        emit_projection(PROJ_EARLY * (b + 1) // NB - PROJ_EARLY * b // NB)

    row = lax.broadcasted_iota(jnp.int32, (CHUNK, LANES), 0)
    lane = lax.broadcasted_iota(jnp.int32, (CHUNK, LANES), 1)
    s_idx = lane % CHUNK
    lo1 = lane < CHUNK
    lo2 = (lax.broadcasted_iota(jnp.int32, (CHUNK, 2 * LANES), 1) % LANES) < CHUNK
    incl = s_idx <= row
    strict = s_idx < row
    u_mask = (row > s_idx).astype(f32)
    eye = (row == s_idx).astype(f32)
    same = lambda n: (row // n) == (s_idx // n)
    blk8, blk16, blk32 = same(8), same(16), same(32)
    r3 = lax.broadcasted_iota(jnp.int32, (CHUNK, 3 * CHUNK), 0)
    c3 = lax.broadcasted_iota(jnp.int32, (CHUNK, 3 * CHUNK), 1) % CHUNK
    tri3 = (c3 <= r3).astype(bf16)
    ones_blk = jnp.ones((CHUNK, LANES), f32)

    def prefix(w_ext):
        hi, mid, lo = _split3(w_ext)
        return dotf(tri3, jnp.concatenate([hi, mid, lo], axis=0))

    def pp(a, b_):
        return _dot(a, _bd(b_, lo1))

    def mlstm_instance(b, p):
        rows = slice(b * CHUNK, (b + 1) * CHUNK)
        ln = slice(p * LANES, (p + 1) * LANES)
        q, k, v = (cur[rows, s * GROUP_W + p * LANES:s * GROUP_W + (p + 1) * LANES] for s in (S_MQ, S_MK, S_MV))
        cb_i = cb_ref[rows, (2 * p) * LANES:(2 * p + 1) * LANES]
        cb_lf = cb_ref[rows, (2 * p + 1) * LANES:(2 * p + 2) * LANES]
        pre = prefix(jnp.concatenate([cb_lf * u_mask + cb_i * eye, cb_lf], axis=1))
        qk = _dot_nt(q, _bd(k, lo1))
        yield
        log_d, cb_b = pre[:, :LANES], pre[:, LANES:]
        m_prev = mm_ref[b, p, 0:1, :]
        mmax = jnp.maximum(m_prev, _cummax_rows(cb_i - cb_b, row))
        m_t = cb_b + mmax
        s = qk * jnp.exp(jnp.where(incl, log_d - m_t, -jnp.inf))
        inter = jnp.exp(m_prev - mmax)
        v_ext = jnp.concatenate([v, ones_blk], axis=1)
        c_ext = mc_ref[b, p]
        res = _dot(jnp.concatenate([s, inter * q], axis=1),
                   jnp.concatenate([_bd(v_ext, lo2), _bd(c_ext, lo2)], axis=0))
        m_new = m_t[CHUNK - 1:CHUNK, :]
        b_last = cb_b[CHUNK - 1:CHUNK, :]
        w = jnp.exp(b_last - cb_b + cb_i - m_new)
        decay = jnp.exp(b_last + m_prev - m_new)
        upd = _dot((k * w).T, v_ext)
        yield
        num, den = res[:, :LANES], res[:, LANES:]
        hm_ref[rows, ln] = num / jnp.maximum(jnp.abs(den), jnp.exp(-m_t))
        mc_ref[b, p] = jnp.concatenate([decay, decay], axis=1) * c_ext + _fold(upd, lo2)
        mm_ref[b, p] = jnp.broadcast_to(m_new, (SUBLANES, LANES))

    def gdn_instance(b, p):
        rows = slice(b * CHUNK, (b + 1) * CHUNK)
        ln = slice(p * LANES, (p + 1) * LANES)
        gq = gqkv_ref[rows, p * LANES:(p + 1) * LANES]
        gk = gqkv_ref[rows, GROUP_W + p * LANES: GROUP_W + (p + 1) * LANES]
        gv = gqkv_ref[rows, 2 * GROUP_W + p * LANES: 2 * GROUP_W + (p + 1) * LANES]
        cb_beta = cb_ref[rows, (4 + 2 * p) * LANES:(5 + 2 * p) * LANES]
        cb_la = cb_ref[rows, (5 + 2 * p) * LANES:(6 + 2 * p) * LANES]
        pre = prefix(jnp.concatenate([cb_la * u_mask, cb_la], axis=1))
        kq = _dot_nt(jnp.concatenate([gk, gq], axis=0), _bd(gk, lo1))
        yield
        diff, cb_g = pre[:, :LANES], pre[:, LANES:]
        dec = jnp.exp(jnp.where(incl, diff, -jnp.inf))
        a = cb_beta * kq[:CHUNK] * jnp.where(strict, dec, 0.0)
        qkd = kq[CHUNK:] * dec
        a0 = jnp.where(blk8, a, 0.0)
        p1 = pp(a0, a0)
        yield
        tn = eye - a0
        tn = tn + pp(tn, p1)
        p2 = pp(p1, p1)
        yield
        tn = tn + pp(tn, p2)
        yield
        for inside, outside in ((blk16, blk8), (blk32, blk16), (None, blk32)):
            off = jnp.where(outside, 0.0, a) if inside is None else jnp.where(inside & ~outside, a, 0.0)
            m_off = pp(off, tn)
            yield
            tn = tn - pp(tn, m_off)
            yield
        eg = jnp.exp(cb_g)
        rhs = jnp.concatenate([cb_beta * gv, cb_beta * eg * gk], axis=1)
        sol = _dot(tn, _bd(rhs, lo2))
        yield
        u_c, wk_c = sol[:, :LANES], sol[:, LANES:]
        g_last = cb_g[CHUNK - 1:CHUNK, :]
        kg = gk * jnp.exp(g_last - cb_g)
        st = gs_ref[b, p]
        st_bd = _bd(st, lo1)
        w_c = u_c - _dot(wk_c, st_bd)
        yield
        hg_ref[rows, ln] = _dot(jnp.concatenate([gq * eg, qkd], axis=1),
                                jnp.concatenate([st_bd, _bd(w_c, lo1)], axis=0))
        gs_ref[b, p] = jnp.exp(g_last) * st + _fold(_dot(kg.T, w_c), lo1)

    live = [gdn_instance(b, p) for b in range(NB) for p in range(2)]
    live += [mlstm_instance(b, p) for b in range(NB) for p in range(2)]
    done = object()
    n_late = len(pending)
    stage = 0
    conv_done = CONV_EARLY
    while live:
        live = [g for g in live if next(g, done) is not done]
        stage += 1
        while conv_done < min(NB, CONV_EARLY + stage * (NB - CONV_EARLY) // SPREAD_STAGES):
            next(conv)
            conv_done += 1
        emit_projection(min(n_late, stage * n_late // SPREAD_STAGES) - (n_late - len(pending)))
    for _ in conv:
        pass
    emit_projection(len(pending))

    hm = hm_ref[...]
    hm = hm * lax.rsqrt(_group_mean_sq(hm, ones4) + EPS) * mng_ref[...]
    y_ref[:, 0:GROUP_W] = (hm * jax.nn.sigmoid(sect(S_MO))).astype(bf16)
    hg = hg_ref[...]
    hg = hg * lax.rsqrt(_group_mean_sq(hg, ones4) + EPS) * gng_ref[...]
    y_ref[:, GROUP_W:2 * GROUP_W] = (hg * _silu(sect(S_GZ))).astype(bf16)
    o_ref[...] += dotf(y_ref[:, 0:2 * GROUP_W], wout_ref[0:2 * GROUP_W, :]).reshape(NB, CHUNK, D_MODEL)


def _const_spec(shape):
    return pl.BlockSpec(shape, lambda *_: (0,) * len(shape))


def _layer_spec(arr, layer, single_buffer=False):
    mode = dict(pipeline_mode=pl.Buffered(1)) if single_buffer else {}
    return pl.BlockSpec((None,) + arr.shape[1:], lambda *_: (layer, 0, 0), **mode)


def _mixer_call(x, layer, sel, n1g, win, gvec, mng, gcw, gng, cdw, cvec, scw, wout):
    bsz, seq, _ = x.shape
    n_t = seq // CHUNK
    grid = (bsz // NB, n_t)
    xspec = pl.BlockSpec((NB, CHUNK, D_MODEL), lambda b, t: (b, t, 0))
    xnext = pl.BlockSpec((NB, CHUNK, D_MODEL), lambda b, t: (b, jnp.minimum(t + 1, n_t - 1), 0))
    lspec = functools.partial(_layer_spec, layer=layer)
    in_specs = [xspec, xnext, lspec(n1g), lspec(win, single_buffer=True), lspec(gvec), _const_spec(sel.shape),
                lspec(mng), lspec(gcw), lspec(gng), lspec(cdw), lspec(cvec), lspec(scw),
                lspec(wout, single_buffer=True)]
    return pl.pallas_call(
        _mixer_kernel,
        grid=grid,
        in_specs=in_specs,
        out_specs=xspec,
        out_shape=jax.ShapeDtypeStruct(x.shape, f32),
        scratch_shapes=[
            pltpu.VMEM((ROWS, PROJ_PAD), f32),
            pltpu.VMEM((ROWS, PROJ_PAD), f32),
            pltpu.VMEM((ROWS, D_MODEL), bf16),
            pltpu.VMEM((ROWS, N_CB * LANES), f32),
            pltpu.VMEM((GDN_CONV - 1, NB, CHUNK + SUBLANES, 3 * GROUP_W), f32),
            pltpu.VMEM((ROWS, 3 * GROUP_W), f32),
            pltpu.VMEM((SUBLANES, NB, CONF_HALO + CHUNK, GROUP_W), f32),
            pltpu.VMEM((SC_CONV - 1, NB, CHUNK + SUBLANES, GROUP_W), f32),
            pltpu.VMEM((ROWS, D_MODEL), bf16),
            pltpu.VMEM((ROWS, GROUP_W), f32),
            pltpu.VMEM((ROWS, GROUP_W), f32),
            pltpu.VMEM((NB, 2, CHUNK, 2 * LANES), f32),
            pltpu.VMEM((NB, 2, SUBLANES, LANES), f32),
            pltpu.VMEM((NB, 2, CHUNK, LANES), f32),
        ],
        compiler_params=pltpu.CompilerParams(dimension_semantics=("arbitrary", "arbitrary"),
                                             vmem_limit_bytes=VMEM_LIMIT),
        name="mixer",
    )(x, x, n1g, win, gvec, sel, mng, gcw, gng, cdw, cvec, scw, wout)


def _ffn_kernel(x_ref, g_ref, wgu_ref, wd_ref, fg_ref, o_ref, *, final):
    x = x_ref[...]
    ms = jnp.mean(x * x, axis=-1, keepdims=True)
    h = (x * lax.rsqrt(ms + EPS) * g_ref[...]).astype(bf16)
    acc = x
    for a, b in FF_CHUNKS:
        gate = jnp.dot(h, wgu_ref[:, a:b], preferred_element_type=f32)
        up = jnp.dot(h, wgu_ref[:, D_FF + a:D_FF + b], preferred_element_type=f32)
        act = (_silu(gate) * up).astype(bf16)
        acc = acc + jnp.dot(act, wd_ref[a:b, :], preferred_element_type=f32)
    if final:
        ms = jnp.mean(acc * acc, axis=-1, keepdims=True)
        acc = acc * lax.rsqrt(ms + EPS) * fg_ref[...]
    o_ref[...] = acc


def _ffn_call(x2d, layer, g, wgu, wd, fg, final):
    n = x2d.shape[0]
    xspec = pl.BlockSpec((TM, D_MODEL), lambda i: (i, 0))
    return pl.pallas_call(
        functools.partial(_ffn_kernel, final=final),
        grid=(n // TM,),
        in_specs=[xspec, _layer_spec(g, layer), _layer_spec(wgu, layer, single_buffer=True),
                  _layer_spec(wd, layer, single_buffer=True), _const_spec(fg.shape)],
        out_specs=xspec,
        out_shape=jax.ShapeDtypeStruct(x2d.shape, f32),
        compiler_params=pltpu.CompilerParams(dimension_semantics=("arbitrary",), vmem_limit_bytes=VMEM_LIMIT),
        name="ffn_final" if final else "ffn",
    )(x2d, g, wgu, wd, fg)


def _sel_matrix():
    sel = np.zeros((LANES, N_CB * LANES), np.float32)
    for blk, (kind, p) in enumerate([(0, 0), (1, 0), (0, 1), (1, 1), (2, 0), (3, 0), (2, 1), (3, 1)]):
        for half in range(2):
            src = 4 * kind + 2 * p + half
            for rep in range(3):
                sel[rep * GATE_REP + src, blk * LANES + half * CHUNK: blk * LANES + (half + 1) * CHUNK] = 1.0
    return jnp.asarray(sel, bf16)


def _pad_rows(a, rows):
    return jnp.pad(a, ((0, 0), (0, rows - a.shape[1]), (0, 0)))


def _prep_w_in(w):
    o = np.cumsum([0, 256, 256, 256, 256, 4, 4, 256, 256, 256, 256, 4, 4, 256, 256, 256, 256, 256])
    w = w.astype(bf16)
    col = lambda i: w[:, :, o[i]:o[i + 1]]
    gates = [col(4), col(5), col(10), col(11)]
    zeros = jnp.zeros(w.shape[:2] + (LANES - 3 * GATE_REP,), bf16)
    sections = [col(6), col(7), col(8),
                col(0) * (HEAD_DIM ** -0.5), col(1), col(2), col(3),
                col(9), col(12), col(13), col(14), col(15), col(16)]
    return jnp.concatenate(sections + gates * 3 + [zeros], axis=2)


def _gate_vec(igate_b, fgate_b, dt_bias, a_log):
    z4 = jnp.zeros_like(igate_b)
    bias = jnp.concatenate([igate_b, fgate_b, z4, dt_bias], axis=1)
    alog = jnp.concatenate([z4, z4, z4, a_log], axis=1)
    pad = jnp.zeros((igate_b.shape[0], LANES - 3 * GATE_REP), f32)
    rows = [jnp.concatenate([v, v, v, pad], axis=1) for v in (bias, alog)]
    return _pad_rows(jnp.stack(rows, axis=1), SUBLANES)


def kernel(x, norm1_g, w_in, m_igate_b, m_fgate_b, m_norm_g, g_conv_w, g_a_log, g_dt_bias, g_norm_g,
           c_dw_w, c_dw_b, c_ln_g, c_ln_b, s_conv_w, w_out, norm2_g, w_gate_up, w_down, final_norm_g):
    bsz, seq, d = x.shape
    depth = w_in.shape[0]
    sel = _sel_matrix()
    mixer_params = (
        norm1_g[:, None, :], _prep_w_in(w_in), _gate_vec(m_igate_b, m_fgate_b, g_dt_bias, g_a_log),
        m_norm_g[:, None, :], _pad_rows(g_conv_w, SUBLANES), jnp.tile(g_norm_g, (1, N_HEADS))[:, None, :],
        _pad_rows(c_dw_w, CONF_HALO), _pad_rows(jnp.stack([c_dw_b, c_ln_g, c_ln_b], axis=1), SUBLANES),
        _pad_rows(s_conv_w, SUBLANES), w_out.astype(bf16))
    ffn_params = (norm2_g[:, None, :], w_gate_up.astype(bf16), w_down.astype(bf16), final_norm_g.reshape(1, d))
    for l in range(depth):
        x = _mixer_call(x, l, sel, *mixer_params)
        x = _ffn_call(x.reshape(bsz * seq, d), l, *ffn_params, final=(l == depth - 1)).reshape(bsz, seq, d)
    return x
```

```python
import functools

import numpy as np
import jax
import jax.numpy as jnp
from jax import lax
from jax.experimental import pallas as pl
from jax.experimental.pallas import tpu as pltpu

D_MODEL = 1024
GROUP_W = 256
HEAD_DIM = 64
N_HEADS = 4
CHUNK = 64
GDN_CONV = 4
CONF_CONV = 31
SC_CONV = 3
D_FF = 2816
EPS = 1e-6

LANES = 128
SUBLANES = 8
N_SECT = 13
GATE_OFF = N_SECT * GROUP_W
PROJ_PAD = GATE_OFF + LANES
GATE_REP = 16
N_CB = 8

S_GQ, S_GK, S_GV, S_MQ, S_MK, S_MV, S_MO, S_GZ, S_CA, S_CG, S_SB, S_SC, S_SX = range(13)

NB = 4
N_SUB = 2
GB = NB * N_SUB
ROWS = NB * CHUNK
PROJ_STEP = 2 * LANES
CONF_HALO = 32
PROJ_EARLY = 4
CONV_EARLY = 2
SPREAD_STAGES = 12
TM = 1024
FF_CHUNKS = ((0, 1024), (1024, 2048), (2048, 2816))
VMEM_LIMIT = 60 * 1024 * 1024

f32 = jnp.float32
bf16 = jnp.bfloat16


def _dot(a, b):
    return jnp.dot(a.astype(bf16), b.astype(bf16), preferred_element_type=f32)


def _dot_nt(a, b):
    return lax.dot_general(a.astype(bf16), b.astype(bf16), (((1,), (1,)), ((), ())), preferred_element_type=f32)


def _split3(x):
    hi = x.astype(bf16)
    r = x - hi.astype(f32)
    mid = r.astype(bf16)
    lo = (r - mid.astype(f32)).astype(bf16)
    return hi, mid, lo


def _softplus(x):
    return jnp.maximum(x, 0.0) + jnp.log1p(jnp.exp(-jnp.abs(x)))


def _silu(x):
    return x * jax.nn.sigmoid(x)


def _bd(y, lo_mask):
    return jnp.concatenate([jnp.where(lo_mask, y, 0.0), jnp.where(lo_mask, 0.0, y)], axis=0)


def _fold(z, lo_mask):
    return jnp.where(lo_mask, z[:CHUNK], z[CHUNK:])


def _cummax_rows(x, row):
    k = 1
    while k < CHUNK:
        x = jnp.maximum(x, jnp.where(row >= k, pltpu.roll(x, k, 0), -jnp.inf))
        k *= 2
    return x


def _group_mean_sq(x, ones_blk):
    return jnp.dot((x * x).astype(bf16), ones_blk, preferred_element_type=f32) * (1.0 / HEAD_DIM)


def _shifted_history_taps(cur, sh_ref, b, w_ref, n_taps, lanes):
    acc = cur[:, lanes] * w_ref[n_taps - 1:n_taps, lanes]
    for j in range(1, n_taps):
        acc = acc + sh_ref[j - 1, b, 0:CHUNK, lanes] * w_ref[n_taps - 1 - j:n_taps - j, lanes]
    return acc


def _store_shifted(cur, sh_ref, b, n_shift):
    for j in range(1, n_shift + 1):
        sh_ref[j - 1, b, j:CHUNK + j, :] = cur


def _roll_history(sh_ref, b, n_shift):
    for j in range(n_shift):
        sh_ref[j, b, 0:SUBLANES, :] = sh_ref[j, b, CHUNK:CHUNK + SUBLANES, :]


N_SUB_BUFS = 7


def _mixer_kernel(xc_ref, xn_ref, n1g_ref, win_ref, gvec_ref, sel_ref, mng_ref, gcw_ref, gng_ref, cdw_ref, cvec_ref,
                  scw_ref, wout_ref, o_ref, *scratch):
    sub_bufs = [scratch[k * N_SUB_BUFS:(k + 1) * N_SUB_BUFS] for k in range(N_SUB)]
    gsh_ref, ush_ref, cxsh_ref, mc_ref, mm_ref, gs_ref = scratch[N_SUB * N_SUB_BUFS:]
    t = pl.program_id(1)
    dotf = functools.partial(jnp.dot, preferred_element_type=f32)

    def norm_rows(x, h_ref):
        ms = jnp.mean(x * x, axis=-1, keepdims=True)
        h_ref[...] = (x * lax.rsqrt(ms + EPS) * n1g_ref[...]).astype(bf16)

    def batch_rows(x_ref, k):
        return x_ref[k * NB:(k + 1) * NB].reshape(ROWS, D_MODEL)

    @pl.when(t == 0)
    def _():
        gsh_ref[...] = jnp.zeros_like(gsh_ref)
        ush_ref[...] = jnp.zeros_like(ush_ref)
        cxsh_ref[...] = jnp.zeros_like(cxsh_ref)
        mc_ref[...] = jnp.zeros_like(mc_ref)
        mm_ref[...] = jnp.zeros_like(mm_ref)
        gs_ref[...] = jnp.zeros_like(gs_ref)
        p0_ref, h0_ref = sub_bufs[0][:2]
        norm_rows(batch_rows(xc_ref, 0), h0_ref)
        p0_ref[...] = dotf(h0_ref[...], win_ref[...])

    lane1 = lax.broadcasted_iota(jnp.int32, (1, LANES), 1)
    gsel = lane1 % GATE_REP
    lane256 = lax.broadcasted_iota(jnp.int32, (GROUP_W, GROUP_W), 1) // HEAD_DIM
    row256 = lax.broadcasted_iota(jnp.int32, (GROUP_W, GROUP_W), 0) // HEAD_DIM
    ones4 = (lane256 == row256).astype(bf16)
    row = lax.broadcasted_iota(jnp.int32, (CHUNK, LANES), 0)
    lane = lax.broadcasted_iota(jnp.int32, (CHUNK, LANES), 1)
    s_idx = lane % CHUNK
    lo1 = lane < CHUNK
    lo2 = (lax.broadcasted_iota(jnp.int32, (CHUNK, 2 * LANES), 1) % LANES) < CHUNK
    incl = s_idx <= row
    strict = s_idx < row
    u_mask = (row > s_idx).astype(f32)
    eye = (row == s_idx).astype(f32)
    same = lambda n: (row // n) == (s_idx // n)
    blk8, blk16, blk32 = same(8), same(16), same(32)
    r3 = lax.broadcasted_iota(jnp.int32, (CHUNK, 3 * CHUNK), 0)
    c3 = lax.broadcasted_iota(jnp.int32, (CHUNK, 3 * CHUNK), 1) % CHUNK
    tri3 = (c3 <= r3).astype(bf16)
    ones_blk = jnp.ones((CHUNK, LANES), f32)
    all256 = slice(0, GROUP_W)
    first = CONF_HALO - (CONF_CONV - 1)

    def prefix(w_ext):
        hi, mid, lo = _split3(w_ext)
        return dotf(tri3, jnp.concatenate([hi, mid, lo], axis=0))

    def pp(a, b_):
        return _dot(a, _bd(b_, lo1))

    def substep(k, x_next):
        cur, _, cb_ref, gqkv_ref, y_ref, hm_ref, hg_ref = sub_bufs[k]
        nxt, h_ref = sub_bufs[(k + 1) % N_SUB][:2]
        b_off = k * NB

        norm_rows(x_next, h_ref)

        def project_cols(c0):
            c1 = min(c0 + PROJ_STEP, PROJ_PAD)
            nxt[:, c0:c1] = dotf(h_ref[...], win_ref[:, c0:c1])

        pending = [functools.partial(project_cols, c0) for c0 in range(0, PROJ_PAD, PROJ_STEP)]

        def emit_projection(n=1):
            for _ in range(min(n, len(pending))):
                pending.pop(0)()

        def sect(s, rows=slice(None)):
            return cur[rows, s * GROUP_W:(s + 1) * GROUP_W]

        def conv_mixers():
            for b in range(NB):
                bb = b_off + b
                rows = slice(b * CHUNK, (b + 1) * CHUNK)
                u = sect(S_CA, rows) * jax.nn.sigmoid(sect(S_CG, rows))
                for r in range(SUBLANES):
                    ush_ref[r, bb, CONF_HALO - r:CONF_HALO - r + CHUNK, :] = u
                acc = jnp.broadcast_to(cvec_ref[0:1, :], (CHUNK, GROUP_W))
                for o in range(first, first + CONF_CONV):
                    r = o % SUBLANES
                    acc = acc + ush_ref[r, bb, o - r:o - r + CHUNK, :] * cdw_ref[o - first:o - first + 1, :]
                for r in range(SUBLANES):
                    ush_ref[r, bb, 0:CONF_HALO, :] = ush_ref[r, bb, CHUNK:CHUNK + CONF_HALO, :]
                mu = jnp.mean(acc, axis=-1, keepdims=True)
                xc = acc - mu
                var = jnp.mean(xc * xc, axis=-1, keepdims=True)
                yc = xc * lax.rsqrt(var + EPS) * cvec_ref[1:2, :] + cvec_ref[2:3, :]
                y_ref[rows, 2 * GROUP_W:3 * GROUP_W] = _silu(yc).astype(bf16)

                cx = sect(S_SC, rows) * sect(S_SX, rows)
                _store_shifted(cx, cxsh_ref, bb, SC_CONV - 1)
                sconv = _shifted_history_taps(cx, cxsh_ref, bb, scw_ref, SC_CONV, all256)
                _roll_history(cxsh_ref, bb, SC_CONV - 1)
                y_ref[rows, 3 * GROUP_W:] = (sect(S_SB, rows) * sconv).astype(bf16)
                if b == NB - 1:
                    part = dotf(y_ref[:, 2 * GROUP_W:], wout_ref[2 * GROUP_W:, :])
                    o_ref[b_off:b_off + NB] = (batch_rows(xc_ref, k) + part).reshape(NB, CHUNK, D_MODEL)
                yield

        conv = conv_mixers()
        for _ in range(CONV_EARLY):
            next(conv)
            emit_projection(2)

        g_pre = cur[:, GATE_OFF:] + gvec_ref[0:1, :]
        a_coef = -jnp.exp(gvec_ref[1:2, :])
        gval = jnp.where(gsel < 4, g_pre,
                         jnp.where(gsel < 8, -_softplus(-g_pre),
                                   jnp.where(gsel < 12, jax.nn.sigmoid(g_pre), a_coef * _softplus(g_pre))))
        g_hi, g_mid, g_lo = _split3(gval)
        pieces = jnp.where(lane1 < GATE_REP, g_hi, jnp.where(lane1 < 2 * GATE_REP, g_mid, g_lo))
        cb_ref[...] = dotf(pieces, sel_ref[...])
        emit_projection()

        for b in range(NB):
            bb = b_off + b
            rows = slice(b * CHUNK, (b + 1) * CHUNK)
            raw = cur[rows, 0:3 * GROUP_W]
            _store_shifted(raw, gsh_ref, bb, GDN_CONV - 1)
            for j in range(3):
                lanes = slice(j * GROUP_W, (j + 1) * GROUP_W)
                v = _silu(_shifted_history_taps(raw, gsh_ref, bb, gcw_ref, GDN_CONV, lanes))
                if j < 2:
                    v = v * lax.rsqrt(_group_mean_sq(v, ones4) * HEAD_DIM + EPS)
                    if j == 0:
                        v = v * (HEAD_DIM ** -0.5)
                gqkv_ref[rows, lanes] = v
            _roll_history(gsh_ref, bb, GDN_CONV - 1)
            emit_projection(PROJ_EARLY * (b + 1) // NB - PROJ_EARLY * b // NB)

        def mlstm_instance(b, p):
            bb = b_off + b
            rows = slice(b * CHUNK, (b + 1) * CHUNK)
            ln = slice(p * LANES, (p + 1) * LANES)
            q, k_, v = (cur[rows, s * GROUP_W + p * LANES:s * GROUP_W + (p + 1) * LANES]
                        for s in (S_MQ, S_MK, S_MV))
            cb_i = cb_ref[rows, (2 * p) * LANES:(2 * p + 1) * LANES]
            cb_lf = cb_ref[rows, (2 * p + 1) * LANES:(2 * p + 2) * LANES]
            pre = prefix(jnp.concatenate([cb_lf * u_mask + cb_i * eye, cb_lf], axis=1))
            qk = _dot_nt(q, _bd(k_, lo1))
            yield
            log_d, cb_b = pre[:, :LANES], pre[:, LANES:]
            m_prev = mm_ref[bb, p, 0:1, :]
            mmax = jnp.maximum(m_prev, _cummax_rows(cb_i - cb_b, row))
            m_t = cb_b + mmax
            s = qk * jnp.exp(jnp.where(incl, log_d - m_t, -jnp.inf))
            inter = jnp.exp(m_prev - mmax)
            v_ext = jnp.concatenate([v, ones_blk], axis=1)
            c_ext = mc_ref[bb, p]
            res = _dot(jnp.concatenate([s, inter * q], axis=1),
                       jnp.concatenate([_bd(v_ext, lo2), _bd(c_ext, lo2)], axis=0))
            m_new = m_t[CHUNK - 1:CHUNK, :]
            b_last = cb_b[CHUNK - 1:CHUNK, :]
            w = jnp.exp(b_last - cb_b + cb_i - m_new)
            decay = jnp.exp(b_last + m_prev - m_new)
            upd = _dot((k_ * w).T, v_ext)
            yield
            num, den = res[:, :LANES], res[:, LANES:]
            hm_ref[rows, ln] = num / jnp.maximum(jnp.abs(den), jnp.exp(-m_t))
            mc_ref[bb, p] = jnp.concatenate([decay, decay], axis=1) * c_ext + _fold(upd, lo2)
            mm_ref[bb, p] = jnp.broadcast_to(m_new, (SUBLANES, LANES))

        def gdn_instance(b, p):
            bb = b_off + b
            rows = slice(b * CHUNK, (b + 1) * CHUNK)
            ln = slice(p * LANES, (p + 1) * LANES)
            gq = gqkv_ref[rows, p * LANES:(p + 1) * LANES]
            gk = gqkv_ref[rows, GROUP_W + p * LANES: GROUP_W + (p + 1) * LANES]
            gv = gqkv_ref[rows, 2 * GROUP_W + p * LANES: 2 * GROUP_W + (p + 1) * LANES]
            cb_beta = cb_ref[rows, (4 + 2 * p) * LANES:(5 + 2 * p) * LANES]
            cb_la = cb_ref[rows, (5 + 2 * p) * LANES:(6 + 2 * p) * LANES]
            pre = prefix(jnp.concatenate([cb_la * u_mask, cb_la], axis=1))
            kq = _dot_nt(jnp.concatenate([gk, gq], axis=0), _bd(gk, lo1))
            yield
            diff, cb_g = pre[:, :LANES], pre[:, LANES:]
            dec = jnp.exp(jnp.where(incl, diff, -jnp.inf))
            a = cb_beta * kq[:CHUNK] * jnp.where(strict, dec, 0.0)
            qkd = kq[CHUNK:] * dec
            a0 = jnp.where(blk8, a, 0.0)
            p1 = pp(a0, a0)
            yield
            tn = eye - a0
            tn = tn + pp(tn, p1)
            p2 = pp(p1, p1)
            yield
            tn = tn + pp(tn, p2)
            yield
            for inside, outside in ((blk16, blk8), (blk32, blk16), (None, blk32)):
                off = jnp.where(outside, 0.0, a) if inside is None else jnp.where(inside & ~outside, a, 0.0)
                m_off = pp(off, tn)
                yield
                tn = tn - pp(tn, m_off)
                yield
            eg = jnp.exp(cb_g)
            rhs = jnp.concatenate([cb_beta * gv, cb_beta * eg * gk], axis=1)
            sol = _dot(tn, _bd(rhs, lo2))
            yield
            u_c, wk_c = sol[:, :LANES], sol[:, LANES:]
            g_last = cb_g[CHUNK - 1:CHUNK, :]
            kg = gk * jnp.exp(g_last - cb_g)
            st = gs_ref[bb, p]
            st_bd = _bd(st, lo1)
            w_c = u_c - _dot(wk_c, st_bd)
            yield
            hg_ref[rows, ln] = _dot(jnp.concatenate([gq * eg, qkd], axis=1),
                                    jnp.concatenate([st_bd, _bd(w_c, lo1)], axis=0))
            gs_ref[bb, p] = jnp.exp(g_last) * st + _fold(_dot(kg.T, w_c), lo1)

        live = [gdn_instance(b, p) for b in range(NB) for p in range(2)]
        live += [mlstm_instance(b, p) for b in range(NB) for p in range(2)]
        done = object()
        n_late = len(pending)
        stage = 0
        conv_done = CONV_EARLY
        while live:
            live = [g for g in live if next(g, done) is not done]
            stage += 1
            while conv_done < min(NB, CONV_EARLY + stage * (NB - CONV_EARLY) // SPREAD_STAGES):
                next(conv)
                conv_done += 1
            emit_projection(min(n_late, stage * n_late // SPREAD_STAGES) - (n_late - len(pending)))
        for _ in conv:
            pass
        emit_projection(len(pending))

        hm = hm_ref[...]
        hm = hm * lax.rsqrt(_group_mean_sq(hm, ones4) + EPS) * mng_ref[...]
        y_ref[:, 0:GROUP_W] = (hm * jax.nn.sigmoid(sect(S_MO))).astype(bf16)
        hg = hg_ref[...]
        hg = hg * lax.rsqrt(_group_mean_sq(hg, ones4) + EPS) * gng_ref[...]
        y_ref[:, GROUP_W:2 * GROUP_W] = (hg * _silu(sect(S_GZ))).astype(bf16)
        part = dotf(y_ref[:, 0:2 * GROUP_W], wout_ref[0:2 * GROUP_W, :])
        o_ref[b_off:b_off + NB] += part.reshape(NB, CHUNK, D_MODEL)

    for k in range(N_SUB):
        substep(k, batch_rows(xc_ref, k + 1) if k + 1 < N_SUB else batch_rows(xn_ref, 0))


def _const_spec(shape):
    return pl.BlockSpec(shape, lambda *_: (0,) * len(shape))


def _layer_spec(arr, layer, single_buffer=False):
    mode = dict(pipeline_mode=pl.Buffered(1)) if single_buffer else {}
    return pl.BlockSpec((None,) + arr.shape[1:], lambda *_: (layer, 0, 0), **mode)


def _mixer_call(x, layer, sel, n1g, win, gvec, mng, gcw, gng, cdw, cvec, scw, wout):
    bsz, seq, _ = x.shape
    n_t = seq // CHUNK
    grid = (bsz // GB, n_t)
    xspec = pl.BlockSpec((GB, CHUNK, D_MODEL), lambda g, t: (g, t, 0))
    xnext = pl.BlockSpec((NB, CHUNK, D_MODEL), lambda g, t: (g * N_SUB, jnp.minimum(t + 1, n_t - 1), 0))
    lspec = functools.partial(_layer_spec, layer=layer)
    in_specs = [xspec, xnext, lspec(n1g), lspec(win, single_buffer=True), lspec(gvec), _const_spec(sel.shape),
                lspec(mng), lspec(gcw), lspec(gng), lspec(cdw), lspec(cvec), lspec(scw),
                lspec(wout, single_buffer=True)]
    per_sub = [
        pltpu.VMEM((ROWS, PROJ_PAD), f32),
        pltpu.VMEM((ROWS, D_MODEL), bf16),
        pltpu.VMEM((ROWS, N_CB * LANES), f32),
        pltpu.VMEM((ROWS, 3 * GROUP_W), f32),
        pltpu.VMEM((ROWS, D_MODEL), bf16),
        pltpu.VMEM((ROWS, GROUP_W), f32),
        pltpu.VMEM((ROWS, GROUP_W), f32),
    ]
    assert len(per_sub) == N_SUB_BUFS
    shared = [
        pltpu.VMEM((GDN_CONV - 1, GB, CHUNK + SUBLANES, 3 * GROUP_W), f32),
        pltpu.VMEM((SUBLANES, GB, CONF_HALO + CHUNK, GROUP_W), f32),
        pltpu.VMEM((SC_CONV - 1, GB, CHUNK + SUBLANES, GROUP_W), f32),
        pltpu.VMEM((GB, 2, CHUNK, 2 * LANES), f32),
        pltpu.VMEM((GB, 2, SUBLANES, LANES), f32),
        pltpu.VMEM((GB, 2, CHUNK, LANES), f32),
    ]
    return pl.pallas_call(
        _mixer_kernel,
        grid=grid,
        in_specs=in_specs,
        out_specs=xspec,
        out_shape=jax.ShapeDtypeStruct(x.shape, f32),
        scratch_shapes=per_sub * N_SUB + shared,
        compiler_params=pltpu.CompilerParams(dimension_semantics=("arbitrary", "arbitrary"),
                                             vmem_limit_bytes=VMEM_LIMIT),
        name="mixer",
    )(x, x, n1g, win, gvec, sel, mng, gcw, gng, cdw, cvec, scw, wout)


def _ffn_kernel(x_ref, g_ref, wgu_ref, wd_ref, fg_ref, o_ref, *, final):
    x = x_ref[...]
    ms = jnp.mean(x * x, axis=-1, keepdims=True)
    h = (x * lax.rsqrt(ms + EPS) * g_ref[...]).astype(bf16)
    acc = x
    for a, b in FF_CHUNKS:
        gate = jnp.dot(h, wgu_ref[:, a:b], preferred_element_type=f32)
        up = jnp.dot(h, wgu_ref[:, D_FF + a:D_FF + b], preferred_element_type=f32)
        act = (_silu(gate) * up).astype(bf16)
        acc = acc + jnp.dot(act, wd_ref[a:b, :], preferred_element_type=f32)
    if final:
        ms = jnp.mean(acc * acc, axis=-1, keepdims=True)
        acc = acc * lax.rsqrt(ms + EPS) * fg_ref[...]
    o_ref[...] = acc


def _ffn_call(x2d, layer, g, wgu, wd, fg, final):
    n = x2d.shape[0]
    xspec = pl.BlockSpec((TM, D_MODEL), lambda i: (i, 0))
    return pl.pallas_call(
        functools.partial(_ffn_kernel, final=final),
        grid=(n // TM,),
        in_specs=[xspec, _layer_spec(g, layer), _layer_spec(wgu, layer, single_buffer=True),
                  _layer_spec(wd, layer, single_buffer=True), _const_spec(fg.shape)],
        out_specs=xspec,
        out_shape=jax.ShapeDtypeStruct(x2d.shape, f32),
        compiler_params=pltpu.CompilerParams(dimension_semantics=("arbitrary",), vmem_limit_bytes=VMEM_LIMIT),
        name="ffn_final" if final else "ffn",
    )(x2d, g, wgu, wd, fg)


def _sel_matrix():
    sel = np.zeros((LANES, N_CB * LANES), np.float32)
    for blk, (kind, p) in enumerate([(0, 0), (1, 0), (0, 1), (1, 1), (2, 0), (3, 0), (2, 1), (3, 1)]):
        for half in range(2):
            src = 4 * kind + 2 * p + half
            for rep in range(3):
                sel[rep * GATE_REP + src, blk * LANES + half * CHUNK: blk * LANES + (half + 1) * CHUNK] = 1.0
    return jnp.asarray(sel, bf16)


def _pad_rows(a, rows):
    return jnp.pad(a, ((0, 0), (0, rows - a.shape[1]), (0, 0)))


def _prep_w_in(w):
    o = np.cumsum([0, 256, 256, 256, 256, 4, 4, 256, 256, 256, 256, 4, 4, 256, 256, 256, 256, 256])
    w = w.astype(bf16)
    col = lambda i: w[:, :, o[i]:o[i + 1]]
    gates = [col(4), col(5), col(10), col(11)]
    zeros = jnp.zeros(w.shape[:2] + (LANES - 3 * GATE_REP,), bf16)
    sections = [col(6), col(7), col(8),
                col(0) * (HEAD_DIM ** -0.5), col(1), col(2), col(3),
                col(9), col(12), col(13), col(14), col(15), col(16)]
    return jnp.concatenate(sections + gates * 3 + [zeros], axis=2)


def _gate_vec(igate_b, fgate_b, dt_bias, a_log):
    z4 = jnp.zeros_like(igate_b)
    bias = jnp.concatenate([igate_b, fgate_b, z4, dt_bias], axis=1)
    alog = jnp.concatenate([z4, z4, z4, a_log], axis=1)
    pad = jnp.zeros((igate_b.shape[0], LANES - 3 * GATE_REP), f32)
    rows = [jnp.concatenate([v, v, v, pad], axis=1) for v in (bias, alog)]
    return _pad_rows(jnp.stack(rows, axis=1), SUBLANES)


def kernel(x, norm1_g, w_in, m_igate_b, m_fgate_b, m_norm_g, g_conv_w, g_a_log, g_dt_bias, g_norm_g,
           c_dw_w, c_dw_b, c_ln_g, c_ln_b, s_conv_w, w_out, norm2_g, w_gate_up, w_down, final_norm_g):
    bsz, seq, d = x.shape
    depth = w_in.shape[0]
    sel = _sel_matrix()
    mixer_params = (
        norm1_g[:, None, :], _prep_w_in(w_in), _gate_vec(m_igate_b, m_fgate_b, g_dt_bias, g_a_log),
        m_norm_g[:, None, :], _pad_rows(g_conv_w, SUBLANES), jnp.tile(g_norm_g, (1, N_HEADS))[:, None, :],
        _pad_rows(c_dw_w, CONF_HALO), _pad_rows(jnp.stack([c_dw_b, c_ln_g, c_ln_b], axis=1), SUBLANES),
        _pad_rows(s_conv_w, SUBLANES), w_out.astype(bf16))
    ffn_params = (norm2_g[:, None, :], w_gate_up.astype(bf16), w_down.astype(bf16), final_norm_g.reshape(1, d))
    for l in range(depth):
        x = _mixer_call(x, l, sel, *mixer_params)
        x = _ffn_call(x.reshape(bsz * seq, d), l, *ffn_params, final=(l == depth - 1)).reshape(bsz, seq, d)
    return x
```

```python
import functools

import numpy as np
import jax
import jax.numpy as jnp
from jax import lax
from jax.experimental import pallas as pl
from jax.experimental.pallas import tpu as pltpu

D_MODEL = 1024
GROUP_W = 256
HEAD_DIM = 64
N_HEADS = 4
CHUNK = 64
GDN_CONV = 4
CONF_CONV = 31
SC_CONV = 3
D_FF = 2816
EPS = 1e-6

LANES = 128
SUBLANES = 8
N_SECT = 13
GATE_OFF = N_SECT * GROUP_W
PROJ_PAD = GATE_OFF + LANES
GATE_REP = 16
N_CB = 8

S_GQ, S_GK, S_GV, S_MQ, S_MK, S_MV, S_MO, S_GZ, S_CA, S_CG, S_SB, S_SC, S_SX = range(13)

NB = 4
N_SUB = 2
N_TC = 2
GB = NB * N_SUB
ROWS = NB * CHUNK
PROJ_STEP = 2 * LANES
CONF_HALO = 32
CONV_EARLY = 2
SPREAD_STAGES = 12
HEAD_PIECES = 2
HEAD_STAGE_PERIOD = 2
TM = 1024
FF_CHUNKS = ((0, 1024), (1024, 2048), (2048, 2816))
VMEM_LIMIT = 60 * 1024 * 1024

f32 = jnp.float32
bf16 = jnp.bfloat16


def _dot(a, b):
    return jnp.dot(a.astype(bf16), b.astype(bf16), preferred_element_type=f32)


def _dot_nt(a, b):
    return lax.dot_general(a.astype(bf16), b.astype(bf16), (((1,), (1,)), ((), ())), preferred_element_type=f32)


def _split3(x):
    hi = x.astype(bf16)
    r = x - hi.astype(f32)
    mid = r.astype(bf16)
    lo = (r - mid.astype(f32)).astype(bf16)
    return hi, mid, lo


def _softplus(x):
    return jnp.maximum(x, 0.0) + jnp.log1p(jnp.exp(-jnp.abs(x)))


def _silu(x):
    return x * jax.nn.sigmoid(x)


def _bd(y, lo_mask):
    return jnp.concatenate([jnp.where(lo_mask, y, 0.0), jnp.where(lo_mask, 0.0, y)], axis=0)


def _fold(z, lo_mask):
    return jnp.where(lo_mask, z[:CHUNK], z[CHUNK:])


def _cummax_rows(x, row):
    k = 1
    while k < CHUNK:
        x = jnp.maximum(x, jnp.where(row >= k, pltpu.roll(x, k, 0), -jnp.inf))
        k *= 2
    return x


def _group_mean_sq(x, ones_blk):
    return jnp.dot((x * x).astype(bf16), ones_blk, preferred_element_type=f32) * (1.0 / HEAD_DIM)


def _shifted_history_taps(cur, sh_ref, b, w_ref, n_taps, lanes):
    acc = cur[:, lanes] * w_ref[n_taps - 1:n_taps, lanes]
    for j in range(1, n_taps):
        acc = acc + sh_ref[j - 1, b, 0:CHUNK, lanes] * w_ref[n_taps - 1 - j:n_taps - j, lanes]
    return acc


def _store_shifted(cur, sh_ref, b, n_shift):
    for j in range(1, n_shift + 1):
        sh_ref[j - 1, b, j:CHUNK + j, :] = cur


def _roll_history(sh_ref, b, n_shift):
    for j in range(n_shift):
        sh_ref[j, b, 0:SUBLANES, :] = sh_ref[j, b, CHUNK:CHUNK + SUBLANES, :]


N_SUB_BUFS = 7
N_PIECES = -(-PROJ_PAD // PROJ_STEP)
DONE = object()


def _mixer_kernel(xc_ref, xn_ref, n1g_ref, win_ref, gvec_ref, sel_ref, mng_ref, gcw_ref, gng_ref, cdw_ref, cvec_ref,
                  scw_ref, wout_ref, o_ref, *scratch):
    sub_bufs = [scratch[k * N_SUB_BUFS:(k + 1) * N_SUB_BUFS] for k in range(N_SUB)]
    gsh_ref, ush_ref, cxsh_ref, mc_ref, mm_ref, gs_ref = scratch[N_SUB * N_SUB_BUFS:]
    t = pl.program_id(1)
    dotf = functools.partial(jnp.dot, preferred_element_type=f32)

    def norm_rows(x, h_ref):
        ms = jnp.mean(x * x, axis=-1, keepdims=True)
        h_ref[...] = (x * lax.rsqrt(ms + EPS) * n1g_ref[...]).astype(bf16)

    def batch_rows(x_ref, k, tc=0):
        return x_ref[k * NB:(k + 1) * NB, tc * CHUNK:(tc + 1) * CHUNK].reshape(ROWS, D_MODEL)

    lane1 = lax.broadcasted_iota(jnp.int32, (1, LANES), 1)
    gsel = lane1 % GATE_REP
    lane256 = lax.broadcasted_iota(jnp.int32, (GROUP_W, GROUP_W), 1) // HEAD_DIM
    row256 = lax.broadcasted_iota(jnp.int32, (GROUP_W, GROUP_W), 0) // HEAD_DIM
    ones4 = (lane256 == row256).astype(bf16)
    row = lax.broadcasted_iota(jnp.int32, (CHUNK, LANES), 0)
    lane = lax.broadcasted_iota(jnp.int32, (CHUNK, LANES), 1)
    s_idx = lane % CHUNK
    lo1 = lane < CHUNK
    lo2 = (lax.broadcasted_iota(jnp.int32, (CHUNK, 2 * LANES), 1) % LANES) < CHUNK
    incl = s_idx <= row
    strict = s_idx < row
    u_mask = (row > s_idx).astype(f32)
    eye = (row == s_idx).astype(f32)
    same = lambda n: (row // n) == (s_idx // n)
    blk8, blk16, blk32 = same(8), same(16), same(32)
    r3 = lax.broadcasted_iota(jnp.int32, (CHUNK, 3 * CHUNK), 0)
    c3 = lax.broadcasted_iota(jnp.int32, (CHUNK, 3 * CHUNK), 1) % CHUNK
    tri3 = (c3 <= r3).astype(bf16)
    ones_blk = jnp.ones((CHUNK, LANES), f32)
    all256 = slice(0, GROUP_W)
    first = CONF_HALO - (CONF_CONV - 1)

    def prefix(w_ext):
        hi, mid, lo = _split3(w_ext)
        return dotf(tri3, jnp.concatenate([hi, mid, lo], axis=0))

    def pp(a, b_):
        return _dot(a, _bd(b_, lo1))

    def front_end(k):
        cur, _, cb_ref, gqkv_ref = sub_bufs[k][:4]
        g_pre = cur[:, GATE_OFF:] + gvec_ref[0:1, :]
        a_coef = -jnp.exp(gvec_ref[1:2, :])
        gval = jnp.where(gsel < 4, g_pre,
                         jnp.where(gsel < 8, -_softplus(-g_pre),
                                   jnp.where(gsel < 12, jax.nn.sigmoid(g_pre), a_coef * _softplus(g_pre))))
        g_hi, g_mid, g_lo = _split3(gval)
        pieces = jnp.where(lane1 < GATE_REP, g_hi, jnp.where(lane1 < 2 * GATE_REP, g_mid, g_lo))
        cb_ref[...] = dotf(pieces, sel_ref[...])
        yield
        for b in range(NB):
            bb = k * NB + b
            rows = slice(b * CHUNK, (b + 1) * CHUNK)
            raw = cur[rows, 0:3 * GROUP_W]
            _store_shifted(raw, gsh_ref, bb, GDN_CONV - 1)
            for j in range(3):
                lanes = slice(j * GROUP_W, (j + 1) * GROUP_W)
                v = _silu(_shifted_history_taps(raw, gsh_ref, bb, gcw_ref, GDN_CONV, lanes))
                if j < 2:
                    v = v * lax.rsqrt(_group_mean_sq(v, ones4) * HEAD_DIM + EPS)
                    if j == 0:
                        v = v * (HEAD_DIM ** -0.5)
                gqkv_ref[rows, lanes] = v
            _roll_history(gsh_ref, bb, GDN_CONV - 1)
            yield

    def make_substep(k, tc, x_next):
        cur, _, cb_ref, gqkv_ref, y_ref, hm_ref, hg_ref = sub_bufs[k]
        nxt, h_ref = sub_bufs[(k + 1) % N_SUB][:2]
        b_off = k * NB
        t_rows = slice(tc * CHUNK, (tc + 1) * CHUNK)

        def start_projection():
            norm_rows(x_next, h_ref)

        def project_cols(c0):
            c1 = min(c0 + PROJ_STEP, PROJ_PAD)
            nxt[:, c0:c1] = dotf(h_ref[...], win_ref[:, c0:c1])

        pending = [functools.partial(project_cols, c0) for c0 in range(0, PROJ_PAD, PROJ_STEP)]

        def emit_projection(n=1):
            for _ in range(min(n, len(pending))):
                pending.pop(0)()

        def sect(s, rows=slice(None)):
            return cur[rows, s * GROUP_W:(s + 1) * GROUP_W]

        def conv_mixers():
            for b in range(NB):
                bb = b_off + b
                rows = slice(b * CHUNK, (b + 1) * CHUNK)
                u = sect(S_CA, rows) * jax.nn.sigmoid(sect(S_CG, rows))
                for r in range(SUBLANES):
                    ush_ref[r, bb, CONF_HALO - r:CONF_HALO - r + CHUNK, :] = u
                acc = jnp.broadcast_to(cvec_ref[0:1, :], (CHUNK, GROUP_W))
                for o in range(first, first + CONF_CONV):
                    r = o % SUBLANES
                    acc = acc + ush_ref[r, bb, o - r:o - r + CHUNK, :] * cdw_ref[o - first:o - first + 1, :]
                for r in range(SUBLANES):
                    ush_ref[r, bb, 0:CONF_HALO, :] = ush_ref[r, bb, CHUNK:CHUNK + CONF_HALO, :]
                mu = jnp.mean(acc, axis=-1, keepdims=True)
                xc = acc - mu
                var = jnp.mean(xc * xc, axis=-1, keepdims=True)
                yc = xc * lax.rsqrt(var + EPS) * cvec_ref[1:2, :] + cvec_ref[2:3, :]
                y_ref[rows, 2 * GROUP_W:3 * GROUP_W] = _silu(yc).astype(bf16)

                cx = sect(S_SC, rows) * sect(S_SX, rows)
                _store_shifted(cx, cxsh_ref, bb, SC_CONV - 1)
                sconv = _shifted_history_taps(cx, cxsh_ref, bb, scw_ref, SC_CONV, all256)
                _roll_history(cxsh_ref, bb, SC_CONV - 1)
                y_ref[rows, 3 * GROUP_W:] = (sect(S_SB, rows) * sconv).astype(bf16)
                if b == NB - 1:
                    part = dotf(y_ref[:, 2 * GROUP_W:], wout_ref[2 * GROUP_W:, :])
                    x_res = batch_rows(xc_ref, k, tc)
                    o_ref[b_off:b_off + NB, t_rows] = (x_res + part).reshape(NB, CHUNK, D_MODEL)
                yield

        conv = conv_mixers()

        def head():
            for _ in range(CONV_EARLY):
                next(conv)
                yield
            yield from front_end(k)

        def mlstm_instance(b, p):
            bb = b_off + b
            rows = slice(b * CHUNK, (b + 1) * CHUNK)
            ln = slice(p * LANES, (p + 1) * LANES)
            q, k_, v = (cur[rows, s * GROUP_W + p * LANES:s * GROUP_W + (p + 1) * LANES]
                        for s in (S_MQ, S_MK, S_MV))
            cb_i = cb_ref[rows, (2 * p) * LANES:(2 * p + 1) * LANES]
            cb_lf = cb_ref[rows, (2 * p + 1) * LANES:(2 * p + 2) * LANES]
            pre = prefix(jnp.concatenate([cb_lf * u_mask + cb_i * eye, cb_lf], axis=1))
            qk = _dot_nt(q, _bd(k_, lo1))
            yield
            log_d, cb_b = pre[:, :LANES], pre[:, LANES:]
            m_prev = mm_ref[bb, p, 0:1, :]
            mmax = jnp.maximum(m_prev, _cummax_rows(cb_i - cb_b, row))
            m_t = cb_b + mmax
            s = qk * jnp.exp(jnp.where(incl, log_d - m_t, -jnp.inf))
            inter = jnp.exp(m_prev - mmax)
            v_ext = jnp.concatenate([v, ones_blk], axis=1)
            c_ext = mc_ref[bb, p]
            res = _dot(jnp.concatenate([s, inter * q], axis=1),
                       jnp.concatenate([_bd(v_ext, lo2), _bd(c_ext, lo2)], axis=0))
            m_new = m_t[CHUNK - 1:CHUNK, :]
            b_last = cb_b[CHUNK - 1:CHUNK, :]
            w = jnp.exp(b_last - cb_b + cb_i - m_new)
            decay = jnp.exp(b_last + m_prev - m_new)
            upd = _dot((k_ * w).T, v_ext)
            yield
            num, den = res[:, :LANES], res[:, LANES:]
            hm_ref[rows, ln] = num / jnp.maximum(jnp.abs(den), jnp.exp(-m_t))
            mc_ref[bb, p] = jnp.concatenate([decay, decay], axis=1) * c_ext + _fold(upd, lo2)
            mm_ref[bb, p] = jnp.broadcast_to(m_new, (SUBLANES, LANES))

        def gdn_instance(b, p):
            bb = b_off + b
            rows = slice(b * CHUNK, (b + 1) * CHUNK)
            ln = slice(p * LANES, (p + 1) * LANES)
            gq = gqkv_ref[rows, p * LANES:(p + 1) * LANES]
            gk = gqkv_ref[rows, GROUP_W + p * LANES: GROUP_W + (p + 1) * LANES]
            gv = gqkv_ref[rows, 2 * GROUP_W + p * LANES: 2 * GROUP_W + (p + 1) * LANES]
            cb_beta = cb_ref[rows, (4 + 2 * p) * LANES:(5 + 2 * p) * LANES]
            cb_la = cb_ref[rows, (5 + 2 * p) * LANES:(6 + 2 * p) * LANES]
            pre = prefix(jnp.concatenate([cb_la * u_mask, cb_la], axis=1))
            kq = _dot_nt(jnp.concatenate([gk, gq], axis=0), _bd(gk, lo1))
            yield
            diff, cb_g = pre[:, :LANES], pre[:, LANES:]
            dec = jnp.exp(jnp.where(incl, diff, -jnp.inf))
            a = cb_beta * kq[:CHUNK] * jnp.where(strict, dec, 0.0)
            qkd = kq[CHUNK:] * dec
            a0 = jnp.where(blk8, a, 0.0)
            p1 = pp(a0, a0)
            yield
            tn = eye - a0
            tn = tn + pp(tn, p1)
            p2 = pp(p1, p1)
            yield
            tn = tn + pp(tn, p2)
            yield
            for inside, outside in ((blk16, blk8), (blk32, blk16), (None, blk32)):
                off = jnp.where(outside, 0.0, a) if inside is None else jnp.where(inside & ~outside, a, 0.0)
                m_off = pp(off, tn)
                yield
                tn = tn - pp(tn, m_off)
                yield
            eg = jnp.exp(cb_g)
            rhs = jnp.concatenate([cb_beta * gv, cb_beta * eg * gk], axis=1)
            sol = _dot(tn, _bd(rhs, lo2))
            yield
            u_c, wk_c = sol[:, :LANES], sol[:, LANES:]
            g_last = cb_g[CHUNK - 1:CHUNK, :]
            kg = gk * jnp.exp(g_last - cb_g)
            st = gs_ref[bb, p]
            st_bd = _bd(st, lo1)
            w_c = u_c - _dot(wk_c, st_bd)
            yield
            hg_ref[rows, ln] = _dot(jnp.concatenate([gq * eg, qkd], axis=1),
                                    jnp.concatenate([st_bd, _bd(w_c, lo1)], axis=0))
            gs_ref[bb, p] = jnp.exp(g_last) * st + _fold(_dot(kg.T, w_c), lo1)

        def rounds():
            live = [gdn_instance(b, p) for b in range(NB) for p in range(2)]
            live += [mlstm_instance(b, p) for b in range(NB) for p in range(2)]
            stage = 0
            conv_done = CONV_EARLY
            while live:
                live = [g for g in live if next(g, DONE) is not DONE]
                stage += 1
                while conv_done < min(NB, CONV_EARLY + stage * (NB - CONV_EARLY) // SPREAD_STAGES):
                    next(conv)
                    conv_done += 1
                yield
            for _ in conv:
                pass

        def tail():
            hm = hm_ref[...]
            hm = hm * lax.rsqrt(_group_mean_sq(hm, ones4) + EPS) * mng_ref[...]
            y_ref[:, 0:GROUP_W] = (hm * jax.nn.sigmoid(sect(S_MO))).astype(bf16)
            hg = hg_ref[...]
            hg = hg * lax.rsqrt(_group_mean_sq(hg, ones4) + EPS) * gng_ref[...]
            y_ref[:, GROUP_W:2 * GROUP_W] = (hg * _silu(sect(S_GZ))).astype(bf16)
            part = dotf(y_ref[:, 0:2 * GROUP_W], wout_ref[0:2 * GROUP_W, :])
            o_ref[b_off:b_off + NB, t_rows] += part.reshape(NB, CHUNK, D_MODEL)

        return start_projection, emit_projection, head, rounds, tail

    @pl.when(t == 0)
    def _():
        gsh_ref[...] = jnp.zeros_like(gsh_ref)
        ush_ref[...] = jnp.zeros_like(ush_ref)
        cxsh_ref[...] = jnp.zeros_like(cxsh_ref)
        mc_ref[...] = jnp.zeros_like(mc_ref)
        mm_ref[...] = jnp.zeros_like(mm_ref)
        gs_ref[...] = jnp.zeros_like(gs_ref)
        p0_ref, h0_ref = sub_bufs[0][:2]
        norm_rows(batch_rows(xc_ref, 0), h0_ref)
        p0_ref[...] = dotf(h0_ref[...], win_ref[...])

    order = [(k, tc) for tc in range(N_TC) for k in range(N_SUB)]
    for i, (k, tc) in enumerate(order):
        x_next = batch_rows(xc_ref, *order[i + 1]) if i + 1 < len(order) else batch_rows(xn_ref, 0)
        start_projection, emit_projection, head, rounds, tail = make_substep(k, tc, x_next)
        start_projection()
        n_late = N_PIECES
        for i, _ in enumerate(head()):
            n_head = HEAD_PIECES if i < CONV_EARLY else 1
            emit_projection(n_head)
            n_late -= n_head
        for stage, _ in enumerate(rounds(), 1):
            emit_projection(1 if stage * n_late // SPREAD_STAGES > (stage - 1) * n_late // SPREAD_STAGES else 0)
        emit_projection(N_PIECES)
        tail()


def _const_spec(shape):
    return pl.BlockSpec(shape, lambda *_: (0,) * len(shape))


def _layer_spec(arr, layer, single_buffer=False):
    mode = dict(pipeline_mode=pl.Buffered(1)) if single_buffer else {}
    return pl.BlockSpec((None,) + arr.shape[1:], lambda *_: (layer, 0, 0), **mode)


def _mixer_call(x, layer, sel, n1g, win, gvec, mng, gcw, gng, cdw, cvec, scw, wout):
    bsz, seq, _ = x.shape
    n_t = seq // CHUNK
    grid = (bsz // GB, n_t // N_TC)
    xspec = pl.BlockSpec((GB, N_TC * CHUNK, D_MODEL), lambda g, t: (g, t, 0))
    xnext = pl.BlockSpec((NB, CHUNK, D_MODEL), lambda g, t: (g * N_SUB, jnp.minimum(N_TC * (t + 1), n_t - 1), 0))
    lspec = functools.partial(_layer_spec, layer=layer)
    in_specs = [xspec, xnext, lspec(n1g), lspec(win, single_buffer=True), lspec(gvec), _const_spec(sel.shape),
                lspec(mng), lspec(gcw), lspec(gng), lspec(cdw), lspec(cvec), lspec(scw),
                lspec(wout, single_buffer=True)]
    per_sub = [
        pltpu.VMEM((ROWS, PROJ_PAD), f32),
        pltpu.VMEM((ROWS, D_MODEL), bf16),
        pltpu.VMEM((ROWS, N_CB * LANES), f32),
        pltpu.VMEM((ROWS, 3 * GROUP_W), f32),
        pltpu.VMEM((ROWS, D_MODEL), bf16),
        pltpu.VMEM((ROWS, GROUP_W), f32),
        pltpu.VMEM((ROWS, GROUP_W), f32),
    ]
    assert len(per_sub) == N_SUB_BUFS
    shared = [
        pltpu.VMEM((GDN_CONV - 1, GB, CHUNK + SUBLANES, 3 * GROUP_W), f32),
        pltpu.VMEM((SUBLANES, GB, CONF_HALO + CHUNK, GROUP_W), f32),
        pltpu.VMEM((SC_CONV - 1, GB, CHUNK + SUBLANES, GROUP_W), f32),
        pltpu.VMEM((GB, 2, CHUNK, 2 * LANES), f32),
        pltpu.VMEM((GB, 2, SUBLANES, LANES), f32),
        pltpu.VMEM((GB, 2, CHUNK, LANES), f32),
    ]
    return pl.pallas_call(
        _mixer_kernel,
        grid=grid,
        in_specs=in_specs,
        out_specs=xspec,
        out_shape=jax.ShapeDtypeStruct(x.shape, f32),
        scratch_shapes=per_sub * N_SUB + shared,
        compiler_params=pltpu.CompilerParams(dimension_semantics=("arbitrary", "arbitrary"),
                                             vmem_limit_bytes=VMEM_LIMIT),
        name="mixer",
    )(x, x, n1g, win, gvec, sel, mng, gcw, gng, cdw, cvec, scw, wout)


def _ffn_kernel(x_ref, g_ref, wgu_ref, wd_ref, fg_ref, o_ref, *, final):
    x = x_ref[...]
    ms = jnp.mean(x * x, axis=-1, keepdims=True)
    h = (x * lax.rsqrt(ms + EPS) * g_ref[...]).astype(bf16)
    acc = x
    for a, b in FF_CHUNKS:
        gate = jnp.dot(h, wgu_ref[:, a:b], preferred_element_type=f32)
        up = jnp.dot(h, wgu_ref[:, D_FF + a:D_FF + b], preferred_element_type=f32)
        act = (_silu(gate) * up).astype(bf16)
        acc = acc + jnp.dot(act, wd_ref[a:b, :], preferred_element_type=f32)
    if final:
        ms = jnp.mean(acc * acc, axis=-1, keepdims=True)
        acc = acc * lax.rsqrt(ms + EPS) * fg_ref[...]
    o_ref[...] = acc


def _ffn_call(x2d, layer, g, wgu, wd, fg, final):
    n = x2d.shape[0]
    xspec = pl.BlockSpec((TM, D_MODEL), lambda i: (i, 0))
    return pl.pallas_call(
        functools.partial(_ffn_kernel, final=final),
        grid=(n // TM,),
        in_specs=[xspec, _layer_spec(g, layer), _layer_spec(wgu, layer, single_buffer=True),
                  _layer_spec(wd, layer, single_buffer=True), _const_spec(fg.shape)],
        out_specs=xspec,
        out_shape=jax.ShapeDtypeStruct(x2d.shape, f32),
        compiler_params=pltpu.CompilerParams(dimension_semantics=("arbitrary",), vmem_limit_bytes=VMEM_LIMIT),
        name="ffn_final" if final else "ffn",
    )(x2d, g, wgu, wd, fg)


def _sel_matrix():
    sel = np.zeros((LANES, N_CB * LANES), np.float32)
    for blk, (kind, p) in enumerate([(0, 0), (1, 0), (0, 1), (1, 1), (2, 0), (3, 0), (2, 1), (3, 1)]):
        for half in range(2):
            src = 4 * kind + 2 * p + half
            for rep in range(3):
                sel[rep * GATE_REP + src, blk * LANES + half * CHUNK: blk * LANES + (half + 1) * CHUNK] = 1.0
    return jnp.asarray(sel, bf16)


def _pad_rows(a, rows):
    return jnp.pad(a, ((0, 0), (0, rows - a.shape[1]), (0, 0)))


def _prep_w_in(w):
    o = np.cumsum([0, 256, 256, 256, 256, 4, 4, 256, 256, 256, 256, 4, 4, 256, 256, 256, 256, 256])
    w = w.astype(bf16)
    col = lambda i: w[:, :, o[i]:o[i + 1]]
    gates = [col(4), col(5), col(10), col(11)]
    zeros = jnp.zeros(w.shape[:2] + (LANES - 3 * GATE_REP,), bf16)
    sections = [col(6), col(7), col(8),
                col(0) * (HEAD_DIM ** -0.5), col(1), col(2), col(3),
                col(9), col(12), col(13), col(14), col(15), col(16)]
    return jnp.concatenate(sections + gates * 3 + [zeros], axis=2)


def _gate_vec(igate_b, fgate_b, dt_bias, a_log):
    z4 = jnp.zeros_like(igate_b)
    bias = jnp.concatenate([igate_b, fgate_b, z4, dt_bias], axis=1)
    alog = jnp.concatenate([z4, z4, z4, a_log], axis=1)
    pad = jnp.zeros((igate_b.shape[0], LANES - 3 * GATE_REP), f32)
    rows = [jnp.concatenate([v, v, v, pad], axis=1) for v in (bias, alog)]
    return _pad_rows(jnp.stack(rows, axis=1), SUBLANES)


def kernel(x, norm1_g, w_in, m_igate_b, m_fgate_b, m_norm_g, g_conv_w, g_a_log, g_dt_bias, g_norm_g,
           c_dw_w, c_dw_b, c_ln_g, c_ln_b, s_conv_w, w_out, norm2_g, w_gate_up, w_down, final_norm_g):
    bsz, seq, d = x.shape
    depth = w_in.shape[0]
    sel = _sel_matrix()
    mixer_params = (
        norm1_g[:, None, :], _prep_w_in(w_in), _gate_vec(m_igate_b, m_fgate_b, g_dt_bias, g_a_log),
        m_norm_g[:, None, :], _pad_rows(g_conv_w, SUBLANES), jnp.tile(g_norm_g, (1, N_HEADS))[:, None, :],
        _pad_rows(c_dw_w, CONF_HALO), _pad_rows(jnp.stack([c_dw_b, c_ln_g, c_ln_b], axis=1), SUBLANES),
        _pad_rows(s_conv_w, SUBLANES), w_out.astype(bf16))
    ffn_params = (norm2_g[:, None, :], w_gate_up.astype(bf16), w_down.astype(bf16), final_norm_g.reshape(1, d))
    for l in range(depth):
        x = _mixer_call(x, l, sel, *mixer_params)
        x = _ffn_call(x.reshape(bsz * seq, d), l, *ffn_params, final=(l == depth - 1)).reshape(bsz, seq, d)
    return x
```

```python
import functools

import numpy as np
import jax
import jax.numpy as jnp
from jax import lax
from jax.experimental import pallas as pl
from jax.experimental.pallas import tpu as pltpu

D_MODEL = 1024
GROUP_W = 256
HEAD_DIM = 64
N_HEADS = 4
CHUNK = 64
GDN_CONV = 4
CONF_CONV = 31
SC_CONV = 3
D_FF = 2816
EPS = 1e-6

LANES = 128
SUBLANES = 8
N_SECT = 13
GATE_OFF = N_SECT * GROUP_W
PROJ_PAD = GATE_OFF + LANES
GATE_REP = 16
N_CB = 8

S_GQ, S_GK, S_GV, S_MQ, S_MK, S_MV, S_MO, S_GZ, S_CA, S_CG, S_SB, S_SC, S_SX = range(13)

NB = 4
N_SUB = 2
GB = NB * N_SUB
ROWS = NB * CHUNK
PROJ_STEP = 2 * LANES
CONF_HALO = 32
PROJ_EARLY = 4
CONV_EARLY = 2
SPREAD_STAGES = 12
TM = 1024
FF_CHUNKS = ((0, 1024), (1024, 2048), (2048, 2816))
VMEM_LIMIT = 60 * 1024 * 1024

f32 = jnp.float32
bf16 = jnp.bfloat16


def _dot(a, b):
    return jnp.dot(a.astype(bf16), b.astype(bf16), preferred_element_type=f32)


def _dot_nt(a, b):
    return lax.dot_general(a.astype(bf16), b.astype(bf16), (((1,), (1,)), ((), ())), preferred_element_type=f32)


def _split3(x):
    hi = x.astype(bf16)
    r = x - hi.astype(f32)
    mid = r.astype(bf16)
    lo = (r - mid.astype(f32)).astype(bf16)
    return hi, mid, lo


def _softplus(x):
    return jnp.maximum(x, 0.0) + jnp.log1p(jnp.exp(-jnp.abs(x)))


def _silu(x):
    return x * jax.nn.sigmoid(x)


def _bd(y, lo_mask):
    return jnp.concatenate([jnp.where(lo_mask, y, 0.0), jnp.where(lo_mask, 0.0, y)], axis=0)


def _fold(z, lo_mask):
    return jnp.where(lo_mask, z[:CHUNK], z[CHUNK:])


def _cummax_rows(x, row):
    k = 1
    while k < CHUNK:
        x = jnp.maximum(x, jnp.where(row >= k, pltpu.roll(x, k, 0), -jnp.inf))
        k *= 2
    return x


def _group_mean_sq(x, ones_blk):
    return jnp.dot((x * x).astype(bf16), ones_blk, preferred_element_type=f32) * (1.0 / HEAD_DIM)


def _shifted_history_taps(cur, sh_ref, b, w_ref, n_taps, lanes):
    acc = cur[:, lanes] * w_ref[n_taps - 1:n_taps, lanes]
    for j in range(1, n_taps):
        acc = acc + sh_ref[j - 1, b, 0:CHUNK, lanes] * w_ref[n_taps - 1 - j:n_taps - j, lanes]
    return acc


def _store_shifted(cur, sh_ref, b, n_shift):
    for j in range(1, n_shift + 1):
        sh_ref[j - 1, b, j:CHUNK + j, :] = cur


def _roll_history(sh_ref, b, n_shift):
    for j in range(n_shift):
        sh_ref[j, b, 0:SUBLANES, :] = sh_ref[j, b, CHUNK:CHUNK + SUBLANES, :]


N_SUB_BUFS = 7


def _mixer_kernel(xc_ref, xn_ref, n1g_ref, win_ref, gvec_ref, sel_ref, mng_ref, gcw_ref, gng_ref, cdw_ref, cvec_ref,
                  scw_ref, wout_ref, o_ref, *scratch):
    sub_bufs = [scratch[k * N_SUB_BUFS:(k + 1) * N_SUB_BUFS] for k in range(N_SUB)]
    gsh_ref, ush_ref, cxsh_ref, mc_ref, mm_ref, gs_ref = scratch[N_SUB * N_SUB_BUFS:]
    t = pl.program_id(1)
    dotf = functools.partial(jnp.dot, preferred_element_type=f32)

    def norm_rows(x, h_ref):
        ms = jnp.mean(x * x, axis=-1, keepdims=True)
        h_ref[...] = (x * lax.rsqrt(ms + EPS) * n1g_ref[...]).astype(bf16)

    def batch_rows(x_ref, k):
        return x_ref[k * NB:(k + 1) * NB].reshape(ROWS, D_MODEL)

    @pl.when(t == 0)
    def _():
        gsh_ref[...] = jnp.zeros_like(gsh_ref)
        ush_ref[...] = jnp.zeros_like(ush_ref)
        cxsh_ref[...] = jnp.zeros_like(cxsh_ref)
        mc_ref[...] = jnp.zeros_like(mc_ref)
        mm_ref[...] = jnp.zeros_like(mm_ref)
        gs_ref[...] = jnp.zeros_like(gs_ref)
        p0_ref, h0_ref = sub_bufs[0][:2]
        norm_rows(batch_rows(xc_ref, 0), h0_ref)
        p0_ref[...] = dotf(h0_ref[...], win_ref[...])

    lane1 = lax.broadcasted_iota(jnp.int32, (1, LANES), 1)
    gsel = lane1 % GATE_REP
    lane256 = lax.broadcasted_iota(jnp.int32, (GROUP_W, GROUP_W), 1) // HEAD_DIM
    row256 = lax.broadcasted_iota(jnp.int32, (GROUP_W, GROUP_W), 0) // HEAD_DIM
    ones4 = (lane256 == row256).astype(bf16)
    row = lax.broadcasted_iota(jnp.int32, (CHUNK, LANES), 0)
    lane = lax.broadcasted_iota(jnp.int32, (CHUNK, LANES), 1)
    s_idx = lane % CHUNK
    lo1 = lane < CHUNK
    lo2 = (lax.broadcasted_iota(jnp.int32, (CHUNK, 2 * LANES), 1) % LANES) < CHUNK
    incl = s_idx <= row
    strict = s_idx < row
    u_mask = (row > s_idx).astype(f32)
    eye = (row == s_idx).astype(f32)
    same = lambda n: (row // n) == (s_idx // n)
    blk8, blk16, blk32 = same(8), same(16), same(32)
    r3 = lax.broadcasted_iota(jnp.int32, (CHUNK, 3 * CHUNK), 0)
    c3 = lax.broadcasted_iota(jnp.int32, (CHUNK, 3 * CHUNK), 1) % CHUNK
    tri3 = (c3 <= r3).astype(bf16)
    ones_blk = jnp.ones((CHUNK, LANES), f32)
    all256 = slice(0, GROUP_W)
    first = CONF_HALO - (CONF_CONV - 1)

    def prefix(w_ext):
        hi, mid, lo = _split3(w_ext)
        return dotf(tri3, jnp.concatenate([hi, mid, lo], axis=0))

    def pp(a, b_):
        return _dot(a, _bd(b_, lo1))

    def substep(k, x_next):
        cur, _, cb_ref, gqkv_ref, y_ref, hm_ref, hg_ref = sub_bufs[k]
        nxt, h_ref = sub_bufs[(k + 1) % N_SUB][:2]
        b_off = k * NB

        norm_rows(x_next, h_ref)

        def project_cols(c0):
            c1 = min(c0 + PROJ_STEP, PROJ_PAD)
            nxt[:, c0:c1] = dotf(h_ref[...], win_ref[:, c0:c1])

        pending = [functools.partial(project_cols, c0) for c0 in range(0, PROJ_PAD, PROJ_STEP)]

        def emit_projection(n=1):
            for _ in range(min(n, len(pending))):
                pending.pop(0)()

        def sect(s, rows=slice(None)):
            return cur[rows, s * GROUP_W:(s + 1) * GROUP_W]

        def conv_mixers():
            for b in range(NB):
                bb = b_off + b
                rows = slice(b * CHUNK, (b + 1) * CHUNK)
                u = sect(S_CA, rows) * jax.nn.sigmoid(sect(S_CG, rows))
                for r in range(SUBLANES):
                    ush_ref[r, bb, CONF_HALO - r:CONF_HALO - r + CHUNK, :] = u
                acc = jnp.broadcast_to(cvec_ref[0:1, :], (CHUNK, GROUP_W))
                for o in range(first, first + CONF_CONV):
                    r = o % SUBLANES
                    acc = acc + ush_ref[r, bb, o - r:o - r + CHUNK, :] * cdw_ref[o - first:o - first + 1, :]
                for r in range(SUBLANES):
                    ush_ref[r, bb, 0:CONF_HALO, :] = ush_ref[r, bb, CHUNK:CHUNK + CONF_HALO, :]
                mu = jnp.mean(acc, axis=-1, keepdims=True)
                xc = acc - mu
                var = jnp.mean(xc * xc, axis=-1, keepdims=True)
                yc = xc * lax.rsqrt(var + EPS) * cvec_ref[1:2, :] + cvec_ref[2:3, :]
                y_ref[rows, 2 * GROUP_W:3 * GROUP_W] = _silu(yc).astype(bf16)

                cx = sect(S_SC, rows) * sect(S_SX, rows)
                _store_shifted(cx, cxsh_ref, bb, SC_CONV - 1)
                sconv = _shifted_history_taps(cx, cxsh_ref, bb, scw_ref, SC_CONV, all256)
                _roll_history(cxsh_ref, bb, SC_CONV - 1)
                y_ref[rows, 3 * GROUP_W:] = (sect(S_SB, rows) * sconv).astype(bf16)
                if b == NB - 1:
                    part = dotf(y_ref[:, 2 * GROUP_W:], wout_ref[2 * GROUP_W:, :])
                    o_ref[b_off:b_off + NB] = (batch_rows(xc_ref, k) + part).reshape(NB, CHUNK, D_MODEL)
                yield

        conv = conv_mixers()
        for _ in range(CONV_EARLY):
            next(conv)
            emit_projection(2)

        g_pre = cur[:, GATE_OFF:] + gvec_ref[0:1, :]
        a_coef = -jnp.exp(gvec_ref[1:2, :])
        gval = jnp.where(gsel < 4, g_pre,
                         jnp.where(gsel < 8, -_softplus(-g_pre),
                                   jnp.where(gsel < 12, jax.nn.sigmoid(g_pre), a_coef * _softplus(g_pre))))
        g_hi, g_mid, g_lo = _split3(gval)
        pieces = jnp.where(lane1 < GATE_REP, g_hi, jnp.where(lane1 < 2 * GATE_REP, g_mid, g_lo))
        cb_ref[...] = dotf(pieces, sel_ref[...])
        emit_projection()

        for b in range(NB):
            bb = b_off + b
            rows = slice(b * CHUNK, (b + 1) * CHUNK)
            raw = cur[rows, 0:3 * GROUP_W]
            _store_shifted(raw, gsh_ref, bb, GDN_CONV - 1)
            for j in range(3):
                lanes = slice(j * GROUP_W, (j + 1) * GROUP_W)
                v = _silu(_shifted_history_taps(raw, gsh_ref, bb, gcw_ref, GDN_CONV, lanes))
                if j < 2:
                    v = v * lax.rsqrt(_group_mean_sq(v, ones4) * HEAD_DIM + EPS)
                    if j == 0:
                        v = v * (HEAD_DIM ** -0.5)
                gqkv_ref[rows, lanes] = v
            _roll_history(gsh_ref, bb, GDN_CONV - 1)
            emit_projection(PROJ_EARLY * (b + 1) // NB - PROJ_EARLY * b // NB)

        def mlstm_instance(b, p):
            bb = b_off + b
            rows = slice(b * CHUNK, (b + 1) * CHUNK)
            ln = slice(p * LANES, (p + 1) * LANES)
            q, k_, v = (cur[rows, s * GROUP_W + p * LANES:s * GROUP_W + (p + 1) * LANES]
                        for s in (S_MQ, S_MK, S_MV))
            cb_i = cb_ref[rows, (2 * p) * LANES:(2 * p + 1) * LANES]
            cb_lf = cb_ref[rows, (2 * p + 1) * LANES:(2 * p + 2) * LANES]
            pre = prefix(jnp.concatenate([cb_lf * u_mask + cb_i * eye, cb_lf], axis=1))
            qk = _dot_nt(q, _bd(k_, lo1))
            yield
            log_d, cb_b = pre[:, :LANES], pre[:, LANES:]
            m_prev = mm_ref[bb, p, 0:1, :]
            mmax = jnp.maximum(m_prev, _cummax_rows(cb_i - cb_b, row))
            m_t = cb_b + mmax
            s = qk * jnp.exp(jnp.where(incl, log_d - m_t, -jnp.inf))
            inter = jnp.exp(m_prev - mmax)
            v_ext = jnp.concatenate([v, ones_blk], axis=1)
            c_ext = mc_ref[bb, p]
            res = _dot(jnp.concatenate([s, inter * q], axis=1),
                       jnp.concatenate([_bd(v_ext, lo2), _bd(c_ext, lo2)], axis=0))
            m_new = m_t[CHUNK - 1:CHUNK, :]
            b_last = cb_b[CHUNK - 1:CHUNK, :]
            w = jnp.exp(b_last - cb_b + cb_i - m_new)
            decay = jnp.exp(b_last + m_prev - m_new)
            upd = _dot((k_ * w).T, v_ext)
            yield
            num, den = res[:, :LANES], res[:, LANES:]
            hm_ref[rows, ln] = num / jnp.maximum(jnp.abs(den), jnp.exp(-m_t))
            mc_ref[bb, p] = jnp.concatenate([decay, decay], axis=1) * c_ext + _fold(upd, lo2)
            mm_ref[bb, p] = jnp.broadcast_to(m_new, (SUBLANES, LANES))

        def gdn_instance(b, p):
            bb = b_off + b
            rows = slice(b * CHUNK, (b + 1) * CHUNK)
            ln = slice(p * LANES, (p + 1) * LANES)
            gq = gqkv_ref[rows, p * LANES:(p + 1) * LANES]
            gk = gqkv_ref[rows, GROUP_W + p * LANES: GROUP_W + (p + 1) * LANES]
            gv = gqkv_ref[rows, 2 * GROUP_W + p * LANES: 2 * GROUP_W + (p + 1) * LANES]
            cb_beta = cb_ref[rows, (4 + 2 * p) * LANES:(5 + 2 * p) * LANES]
            cb_la = cb_ref[rows, (5 + 2 * p) * LANES:(6 + 2 * p) * LANES]
            pre = prefix(jnp.concatenate([cb_la * u_mask, cb_la], axis=1))
            kq = _dot_nt(jnp.concatenate([gk, gq], axis=0), _bd(gk, lo1))
            yield
            diff, cb_g = pre[:, :LANES], pre[:, LANES:]
            dec = jnp.exp(jnp.where(incl, diff, -jnp.inf))
            a = cb_beta * kq[:CHUNK] * jnp.where(strict, dec, 0.0)
            qkd = kq[CHUNK:] * dec
            a0 = jnp.where(blk8, a, 0.0)
            p1 = pp(a0, a0)
            yield
            tn = eye - a0
            tn = tn + pp(tn, p1)
            p2 = pp(p1, p1)
            yield
            tn = tn + pp(tn, p2)
            yield
            for inside, outside in ((blk16, blk8), (blk32, blk16), (None, blk32)):
                off = jnp.where(outside, 0.0, a) if inside is None else jnp.where(inside & ~outside, a, 0.0)
                m_off = pp(off, tn)
                yield
                tn = tn - pp(tn, m_off)
                yield
            eg = jnp.exp(cb_g)
            rhs = jnp.concatenate([cb_beta * gv, cb_beta * eg * gk], axis=1)
            sol = _dot(tn, _bd(rhs, lo2))
            yield
            u_c, wk_c = sol[:, :LANES], sol[:, LANES:]
            g_last = cb_g[CHUNK - 1:CHUNK, :]
            kg = gk * jnp.exp(g_last - cb_g)
            st = gs_ref[bb, p]
            st_bd = _bd(st, lo1)
            w_c = u_c - _dot(wk_c, st_bd)
            yield
            hg_ref[rows, ln] = _dot(jnp.concatenate([gq * eg, qkd], axis=1),
                                    jnp.concatenate([st_bd, _bd(w_c, lo1)], axis=0))
            gs_ref[bb, p] = jnp.exp(g_last) * st + _fold(_dot(kg.T, w_c), lo1)

        live = [gdn_instance(b, p) for b in range(NB) for p in range(2)]
        live += [mlstm_instance(b, p) for b in range(NB) for p in range(2)]
        done = object()
        n_late = len(pending)
        stage = 0
        conv_done = CONV_EARLY
        while live:
            live = [g for g in live if next(g, done) is not done]
            stage += 1
            while conv_done < min(NB, CONV_EARLY + stage * (NB - CONV_EARLY) // SPREAD_STAGES):
                next(conv)
                conv_done += 1
            emit_projection(min(n_late, stage * n_late // SPREAD_STAGES) - (n_late - len(pending)))
        for _ in conv:
            pass
        emit_projection(len(pending))

        hm = hm_ref[...]
        hm = hm * lax.rsqrt(_group_mean_sq(hm, ones4) + EPS) * mng_ref[...]
        y_ref[:, 0:GROUP_W] = (hm * jax.nn.sigmoid(sect(S_MO))).astype(bf16)
        hg = hg_ref[...]
        hg = hg * lax.rsqrt(_group_mean_sq(hg, ones4) + EPS) * gng_ref[...]
        y_ref[:, GROUP_W:2 * GROUP_W] = (hg * _silu(sect(S_GZ))).astype(bf16)
        part = dotf(y_ref[:, 0:2 * GROUP_W], wout_ref[0:2 * GROUP_W, :])
        o_ref[b_off:b_off + NB] += part.reshape(NB, CHUNK, D_MODEL)

    for k in range(N_SUB):
        substep(k, batch_rows(xc_ref, k + 1) if k + 1 < N_SUB else batch_rows(xn_ref, 0))


def _const_spec(shape):
    return pl.BlockSpec(shape, lambda *_: (0,) * len(shape))


def _layer_spec(arr, layer, single_buffer=False):
    mode = dict(pipeline_mode=pl.Buffered(1)) if single_buffer else {}
    return pl.BlockSpec((None,) + arr.shape[1:], lambda *_: (layer, 0, 0), **mode)


def _mixer_call(x, layer, sel, n1g, win, gvec, mng, gcw, gng, cdw, cvec, scw, wout):
    bsz, seq, _ = x.shape
    n_t = seq // CHUNK
    grid = (bsz // GB, n_t)
    xspec = pl.BlockSpec((GB, CHUNK, D_MODEL), lambda g, t: (g, t, 0))
    xnext = pl.BlockSpec((NB, CHUNK, D_MODEL), lambda g, t: (g * N_SUB, jnp.minimum(t + 1, n_t - 1), 0))
    lspec = functools.partial(_layer_spec, layer=layer)
    in_specs = [xspec, xnext, lspec(n1g), lspec(win, single_buffer=True), lspec(gvec), _const_spec(sel.shape),
                lspec(mng), lspec(gcw), lspec(gng), lspec(cdw), lspec(cvec), lspec(scw),
                lspec(wout, single_buffer=True)]
    per_sub = [
        pltpu.VMEM((ROWS, PROJ_PAD), f32),
        pltpu.VMEM((ROWS, D_MODEL), bf16),
        pltpu.VMEM((ROWS, N_CB * LANES), f32),
        pltpu.VMEM((ROWS, 3 * GROUP_W), f32),
        pltpu.VMEM((ROWS, D_MODEL), bf16),
        pltpu.VMEM((ROWS, GROUP_W), f32),
        pltpu.VMEM((ROWS, GROUP_W), f32),
    ]
    assert len(per_sub) == N_SUB_BUFS
    shared = [
        pltpu.VMEM((GDN_CONV - 1, GB, CHUNK + SUBLANES, 3 * GROUP_W), f32),
        pltpu.VMEM((SUBLANES, GB, CONF_HALO + CHUNK, GROUP_W), f32),
        pltpu.VMEM((SC_CONV - 1, GB, CHUNK + SUBLANES, GROUP_W), f32),
        pltpu.VMEM((GB, 2, CHUNK, 2 * LANES), f32),
        pltpu.VMEM((GB, 2, SUBLANES, LANES), f32),
        pltpu.VMEM((GB, 2, CHUNK, LANES), f32),
    ]
    return pl.pallas_call(
        _mixer_kernel,
        grid=grid,
        in_specs=in_specs,
        out_specs=xspec,
        out_shape=jax.ShapeDtypeStruct(x.shape, f32),
        scratch_shapes=per_sub * N_SUB + shared,
        compiler_params=pltpu.CompilerParams(dimension_semantics=("arbitrary", "arbitrary"),
                                             vmem_limit_bytes=VMEM_LIMIT),
        name="mixer",
    )(x, x, n1g, win, gvec, sel, mng, gcw, gng, cdw, cvec, scw, wout)


def _ffn_kernel(x_ref, g_ref, wgu_ref, wd_ref, fg_ref, o_ref, *, final):
    x = x_ref[...]
    ms = jnp.mean(x * x, axis=-1, keepdims=True)
    h = (x * lax.rsqrt(ms + EPS) * g_ref[...]).astype(bf16)
    acc = x
    for a, b in FF_CHUNKS:
        gate = jnp.dot(h, wgu_ref[:, a:b], preferred_element_type=f32)
        up = jnp.dot(h, wgu_ref[:, D_FF + a:D_FF + b], preferred_element_type=f32)
        act = (_silu(gate) * up).astype(bf16)
        acc = acc + jnp.dot(act, wd_ref[a:b, :], preferred_element_type=f32)
    if final:
        ms = jnp.mean(acc * acc, axis=-1, keepdims=True)
        acc = acc * lax.rsqrt(ms + EPS) * fg_ref[...]
    o_ref[...] = acc


def _ffn_call(x2d, layer, g, wgu, wd, fg, final):
    n = x2d.shape[0]
    xspec = pl.BlockSpec((TM, D_MODEL), lambda i: (i, 0))
    return pl.pallas_call(
        functools.partial(_ffn_kernel, final=final),
        grid=(n // TM,),
        in_specs=[xspec, _layer_spec(g, layer), _layer_spec(wgu, layer, single_buffer=True),
                  _layer_spec(wd, layer, single_buffer=True), _const_spec(fg.shape)],
        out_specs=xspec,
        out_shape=jax.ShapeDtypeStruct(x2d.shape, f32),
        compiler_params=pltpu.CompilerParams(dimension_semantics=("arbitrary",), vmem_limit_bytes=VMEM_LIMIT),
        name="ffn_final" if final else "ffn",
    )(x2d, g, wgu, wd, fg)


def _sel_matrix():
    sel = np.zeros((LANES, N_CB * LANES), np.float32)
    for blk, (kind, p) in enumerate([(0, 0), (1, 0), (0, 1), (1, 1), (2, 0), (3, 0), (2, 1), (3, 1)]):
        for half in range(2):
            src = 4 * kind + 2 * p + half
            for rep in range(3):
                sel[rep * GATE_REP + src, blk * LANES + half * CHUNK: blk * LANES + (half + 1) * CHUNK] = 1.0
    return jnp.asarray(sel, bf16)


def _pad_rows(a, rows):
    return jnp.pad(a, ((0, 0), (0, rows - a.shape[1]), (0, 0)))


def _prep_w_in(w):
    m0, g0, c0 = 0, 4 * GROUP_W + 2 * N_HEADS, 8 * GROUP_W + 4 * N_HEADS
    scale = jnp.where(jnp.arange(4 * GROUP_W) < GROUP_W, HEAD_DIM ** -0.5, 1.0).astype(w.dtype)
    mlstm = w[:, :, m0:m0 + 4 * GROUP_W] * scale
    gates = jnp.concatenate([w[:, :, g0 - 2 * N_HEADS:g0], w[:, :, c0 - 2 * N_HEADS:c0]], axis=2)
    zeros = jnp.zeros(w.shape[:2] + (LANES - 3 * GATE_REP,), w.dtype)
    parts = [w[:, :, g0:g0 + 3 * GROUP_W], mlstm, w[:, :, g0 + 3 * GROUP_W:g0 + 4 * GROUP_W],
             w[:, :, c0:c0 + 5 * GROUP_W], gates, gates, gates, zeros]
    return jnp.concatenate(parts, axis=2).astype(bf16)


def _gate_vec(igate_b, fgate_b, dt_bias, a_log):
    z4 = jnp.zeros_like(igate_b)
    bias = jnp.concatenate([igate_b, fgate_b, z4, dt_bias], axis=1)
    alog = jnp.concatenate([z4, z4, z4, a_log], axis=1)
    pad = jnp.zeros((igate_b.shape[0], LANES - 3 * GATE_REP), f32)
    rows = [jnp.concatenate([v, v, v, pad], axis=1) for v in (bias, alog)]
    return _pad_rows(jnp.stack(rows, axis=1), SUBLANES)


def kernel(x, norm1_g, w_in, m_igate_b, m_fgate_b, m_norm_g, g_conv_w, g_a_log, g_dt_bias, g_norm_g,
           c_dw_w, c_dw_b, c_ln_g, c_ln_b, s_conv_w, w_out, norm2_g, w_gate_up, w_down, final_norm_g):
    bsz, seq, d = x.shape
    depth = w_in.shape[0]
    sel = _sel_matrix()
    mixer_params = (
        norm1_g[:, None, :], _prep_w_in(w_in), _gate_vec(m_igate_b, m_fgate_b, g_dt_bias, g_a_log),
        m_norm_g[:, None, :], _pad_rows(g_conv_w, SUBLANES), jnp.tile(g_norm_g, (1, N_HEADS))[:, None, :],
        _pad_rows(c_dw_w, CONF_HALO), _pad_rows(jnp.stack([c_dw_b, c_ln_g, c_ln_b], axis=1), SUBLANES),
        _pad_rows(s_conv_w, SUBLANES), w_out.astype(bf16))
    ffn_params = (norm2_g[:, None, :], w_gate_up.astype(bf16), w_down.astype(bf16), final_norm_g.reshape(1, d))
    for l in range(depth):
        x = _mixer_call(x, l, sel, *mixer_params)
        x = _ffn_call(x.reshape(bsz * seq, d), l, *ffn_params, final=(l == depth - 1)).reshape(bsz, seq, d)
    return x
```

```python
import functools

import numpy as np
import jax
import jax.numpy as jnp
from jax import lax
from jax.experimental import pallas as pl
from jax.experimental.pallas import tpu as pltpu

D_MODEL = 1024
GROUP_W = 256
HEAD_DIM = 64
N_HEADS = 4
CHUNK = 64
GDN_CONV = 4
CONF_CONV = 31
SC_CONV = 3
D_FF = 2816
EPS = 1e-6

LANES = 128
SUBLANES = 8
N_SECT = 13
GATE_OFF = N_SECT * GROUP_W
PROJ_PAD = GATE_OFF + LANES
GATE_REP = 16
N_CB = 8

S_GQ, S_GK, S_GV, S_MQ, S_MK, S_MV, S_MO, S_GZ, S_CA, S_CG, S_SB, S_SC, S_SX = range(13)

NB = 4
N_SUB = 2
GB = NB * N_SUB
ROWS = NB * CHUNK
PROJ_STEP = 2 * LANES
CONF_HALO = 32
PROJ_EARLY = 4
CONV_EARLY = 2
SPREAD_STAGES = 12
TM = 1024
FF_CHUNKS = ((0, 1024), (1024, 2048), (2048, 2816))
VMEM_LIMIT = 60 * 1024 * 1024

f32 = jnp.float32
bf16 = jnp.bfloat16


def _dot(a, b):
    return jnp.dot(a.astype(bf16), b.astype(bf16), preferred_element_type=f32)


def _dot_nt(a, b):
    return lax.dot_general(a.astype(bf16), b.astype(bf16), (((1,), (1,)), ((), ())), preferred_element_type=f32)


def _split3(x):
    hi = x.astype(bf16)
    r = x - hi.astype(f32)
    mid = r.astype(bf16)
    lo = (r - mid.astype(f32)).astype(bf16)
    return hi, mid, lo


def _softplus(x):
    return jnp.maximum(x, 0.0) + jnp.log1p(jnp.exp(-jnp.abs(x)))


def _silu(x):
    return x * jax.nn.sigmoid(x)


def _bd(y, lo_mask):
    return jnp.concatenate([jnp.where(lo_mask, y, 0.0), jnp.where(lo_mask, 0.0, y)], axis=0)


def _fold(z, lo_mask):
    return jnp.where(lo_mask, z[:CHUNK], z[CHUNK:])


def _cummax_rows(x, row):
    k = 1
    while k < CHUNK:
        x = jnp.maximum(x, jnp.where(row >= k, pltpu.roll(x, k, 0), -jnp.inf))
        k *= 2
    return x


def _group_mean_sq(x, ones_blk):
    return jnp.dot((x * x).astype(bf16), ones_blk, preferred_element_type=f32) * (1.0 / HEAD_DIM)


def _shifted_history_taps(cur, sh_ref, b, w_ref, n_taps, lanes):
    acc = cur[:, lanes] * w_ref[n_taps - 1:n_taps, lanes]
    for j in range(1, n_taps):
        acc = acc + sh_ref[j - 1, b, 0:CHUNK, lanes] * w_ref[n_taps - 1 - j:n_taps - j, lanes]
    return acc


def _store_shifted(cur, sh_ref, b, n_shift):
    for j in range(1, n_shift + 1):
        sh_ref[j - 1, b, j:CHUNK + j, :] = cur


def _roll_history(sh_ref, b, n_shift):
    for j in range(n_shift):
        sh_ref[j, b, 0:SUBLANES, :] = sh_ref[j, b, CHUNK:CHUNK + SUBLANES, :]


N_SUB_BUFS = 7


def _mixer_kernel(xc_ref, xn_ref, n1g_ref, win_ref, gvec_ref, sel_ref, mng_ref, gcw_ref, gng_ref, cdw_ref, cvec_ref,
                  scw_ref, wout_ref, o_ref, *scratch):
    sub_bufs = [scratch[k * N_SUB_BUFS:(k + 1) * N_SUB_BUFS] for k in range(N_SUB)]
    gsh_ref, ush_ref, cxsh_ref, mc_ref, mm_ref, gs_ref = scratch[N_SUB * N_SUB_BUFS:]
    t = pl.program_id(1)
    dotf = functools.partial(jnp.dot, preferred_element_type=f32)

    def norm_rows(x, h_ref):
        ms = jnp.mean(x * x, axis=-1, keepdims=True)
        h_ref[...] = (x * lax.rsqrt(ms + EPS) * n1g_ref[...]).astype(bf16)

    def batch_rows(x_ref, k):
        return x_ref[k * NB:(k + 1) * NB].reshape(ROWS, D_MODEL)

    @pl.when(t == 0)
    def _():
        gsh_ref[...] = jnp.zeros_like(gsh_ref)
        ush_ref[...] = jnp.zeros_like(ush_ref)
        cxsh_ref[...] = jnp.zeros_like(cxsh_ref)
        mc_ref[...] = jnp.zeros_like(mc_ref)
        mm_ref[...] = jnp.zeros_like(mm_ref)
        gs_ref[...] = jnp.zeros_like(gs_ref)
        p0_ref, h0_ref = sub_bufs[0][:2]
        norm_rows(batch_rows(xc_ref, 0), h0_ref)
        p0_ref[...] = dotf(h0_ref[...], win_ref[...])

    lane1 = lax.broadcasted_iota(jnp.int32, (1, LANES), 1)
    gsel = lane1 % GATE_REP
    lane256 = lax.broadcasted_iota(jnp.int32, (GROUP_W, GROUP_W), 1) // HEAD_DIM
    row256 = lax.broadcasted_iota(jnp.int32, (GROUP_W, GROUP_W), 0) // HEAD_DIM
    ones4 = (lane256 == row256).astype(bf16)
    row = lax.broadcasted_iota(jnp.int32, (CHUNK, LANES), 0)
    lane = lax.broadcasted_iota(jnp.int32, (CHUNK, LANES), 1)
    s_idx = lane % CHUNK
    lo1 = lane < CHUNK
    lo2 = (lax.broadcasted_iota(jnp.int32, (CHUNK, 2 * LANES), 1) % LANES) < CHUNK
    incl = s_idx <= row
    strict = s_idx < row
    u_mask = (row > s_idx).astype(f32)
    eye = (row == s_idx).astype(f32)
    same = lambda n: (row // n) == (s_idx // n)
    blk8, blk16, blk32 = same(8), same(16), same(32)
    tri2 = (s_idx <= row).astype(bf16)
    ones_blk = jnp.ones((CHUNK, LANES), f32)
    all256 = slice(0, GROUP_W)
    first = CONF_HALO - (CONF_CONV - 1)

    def prefix(w_ext):
        hi = w_ext.astype(bf16)
        lo = (w_ext - hi.astype(f32)).astype(bf16)
        return dotf(tri2, jnp.concatenate([hi, lo], axis=0))

    def pp(a, b_):
        return _dot(a, _bd(b_, lo1))

    def substep(k, x_next):
        cur, _, cb_ref, gqkv_ref, y_ref, hm_ref, hg_ref = sub_bufs[k]
        nxt, h_ref = sub_bufs[(k + 1) % N_SUB][:2]
        b_off = k * NB

        norm_rows(x_next, h_ref)

        def project_cols(c0):
            c1 = min(c0 + PROJ_STEP, PROJ_PAD)
            nxt[:, c0:c1] = dotf(h_ref[...], win_ref[:, c0:c1])

        pending = [functools.partial(project_cols, c0) for c0 in range(0, PROJ_PAD, PROJ_STEP)]

        def emit_projection(n=1):
            for _ in range(min(n, len(pending))):
                pending.pop(0)()

        def sect(s, rows=slice(None)):
            return cur[rows, s * GROUP_W:(s + 1) * GROUP_W]

        def conv_mixers():
            for b in range(NB):
                bb = b_off + b
                rows = slice(b * CHUNK, (b + 1) * CHUNK)
                u = sect(S_CA, rows) * jax.nn.sigmoid(sect(S_CG, rows))
                for r in range(SUBLANES):
                    ush_ref[r, bb, CONF_HALO - r:CONF_HALO - r + CHUNK, :] = u
                acc = jnp.broadcast_to(cvec_ref[0:1, :], (CHUNK, GROUP_W))
                for o in range(first, first + CONF_CONV):
                    r = o % SUBLANES
                    acc = acc + ush_ref[r, bb, o - r:o - r + CHUNK, :] * cdw_ref[o - first:o - first + 1, :]
                for r in range(SUBLANES):
                    ush_ref[r, bb, 0:CONF_HALO, :] = ush_ref[r, bb, CHUNK:CHUNK + CONF_HALO, :]
                mu = jnp.mean(acc, axis=-1, keepdims=True)
                xc = acc - mu
                var = jnp.mean(xc * xc, axis=-1, keepdims=True)
                yc = xc * lax.rsqrt(var + EPS) * cvec_ref[1:2, :] + cvec_ref[2:3, :]
                y_ref[rows, 2 * GROUP_W:3 * GROUP_W] = _silu(yc).astype(bf16)

                cx = sect(S_SC, rows) * sect(S_SX, rows)
                _store_shifted(cx, cxsh_ref, bb, SC_CONV - 1)
                sconv = _shifted_history_taps(cx, cxsh_ref, bb, scw_ref, SC_CONV, all256)
                _roll_history(cxsh_ref, bb, SC_CONV - 1)
                y_ref[rows, 3 * GROUP_W:] = (sect(S_SB, rows) * sconv).astype(bf16)
                if b == NB - 1:
                    part = dotf(y_ref[:, 2 * GROUP_W:], wout_ref[2 * GROUP_W:, :])
                    o_ref[b_off:b_off + NB] = (batch_rows(xc_ref, k) + part).reshape(NB, CHUNK, D_MODEL)
                yield

        conv = conv_mixers()
        for _ in range(CONV_EARLY):
            next(conv)
            emit_projection(2)

        g_pre = cur[:, GATE_OFF:] + gvec_ref[0:1, :]
        a_coef = -jnp.exp(gvec_ref[1:2, :])
        gval = jnp.where(gsel < 4, g_pre,
                         jnp.where(gsel < 8, -_softplus(-g_pre),
                                   jnp.where(gsel < 12, jax.nn.sigmoid(g_pre), a_coef * _softplus(g_pre))))
        g_hi, g_mid, g_lo = _split3(gval)
        pieces = jnp.where(lane1 < GATE_REP, g_hi, jnp.where(lane1 < 2 * GATE_REP, g_mid, g_lo))
        cb_ref[...] = dotf(pieces, sel_ref[...])
        emit_projection()

        for b in range(NB):
            bb = b_off + b
            rows = slice(b * CHUNK, (b + 1) * CHUNK)
            raw = cur[rows, 0:3 * GROUP_W]
            _store_shifted(raw, gsh_ref, bb, GDN_CONV - 1)
            for j in range(3):
                lanes = slice(j * GROUP_W, (j + 1) * GROUP_W)
                v = _silu(_shifted_history_taps(raw, gsh_ref, bb, gcw_ref, GDN_CONV, lanes))
                if j < 2:
                    v = v * lax.rsqrt(_group_mean_sq(v, ones4) * HEAD_DIM + EPS)
                    if j == 0:
                        v = v * (HEAD_DIM ** -0.5)
                gqkv_ref[rows, lanes] = v
            _roll_history(gsh_ref, bb, GDN_CONV - 1)
            emit_projection(PROJ_EARLY * (b + 1) // NB - PROJ_EARLY * b // NB)

        def mlstm_instance(b, p):
            bb = b_off + b
            rows = slice(b * CHUNK, (b + 1) * CHUNK)
            ln = slice(p * LANES, (p + 1) * LANES)
            q, k_, v = (cur[rows, s * GROUP_W + p * LANES:s * GROUP_W + (p + 1) * LANES]
                        for s in (S_MQ, S_MK, S_MV))
            cb_i = cb_ref[rows, (2 * p) * LANES:(2 * p + 1) * LANES]
            cb_lf = cb_ref[rows, (2 * p + 1) * LANES:(2 * p + 2) * LANES]
            pre = prefix(jnp.concatenate([cb_lf * u_mask + cb_i * eye, cb_lf], axis=1))
            qk = _dot_nt(q, _bd(k_, lo1))
            yield
            log_d, cb_b = pre[:, :LANES], pre[:, LANES:]
            m_prev = mm_ref[bb, p, 0:1, :]
            mmax = jnp.maximum(m_prev, _cummax_rows(cb_i - cb_b, row))
            m_t = cb_b + mmax
            s = qk * jnp.exp(jnp.where(incl, log_d - m_t, -jnp.inf))
            inter = jnp.exp(m_prev - mmax)
            v_ext = jnp.concatenate([v, ones_blk], axis=1)
            c_ext = mc_ref[bb, p]
            res = _dot(jnp.concatenate([s, inter * q], axis=1),
                       jnp.concatenate([_bd(v_ext, lo2), _bd(c_ext, lo2)], axis=0))
            m_new = m_t[CHUNK - 1:CHUNK, :]
            b_last = cb_b[CHUNK - 1:CHUNK, :]
            w = jnp.exp(b_last - cb_b + cb_i - m_new)
            decay = jnp.exp(b_last + m_prev - m_new)
            upd = _dot((k_ * w).T, v_ext)
            yield
            num, den = res[:, :LANES], res[:, LANES:]
            hm_ref[rows, ln] = num / jnp.maximum(jnp.abs(den), jnp.exp(-m_t))
            mc_ref[bb, p] = jnp.concatenate([decay, decay], axis=1) * c_ext + _fold(upd, lo2)
            mm_ref[bb, p] = jnp.broadcast_to(m_new, (SUBLANES, LANES))

        def gdn_instance(b, p):
            bb = b_off + b
            rows = slice(b * CHUNK, (b + 1) * CHUNK)
            ln = slice(p * LANES, (p + 1) * LANES)
            gq = gqkv_ref[rows, p * LANES:(p + 1) * LANES]
            gk = gqkv_ref[rows, GROUP_W + p * LANES: GROUP_W + (p + 1) * LANES]
            gv = gqkv_ref[rows, 2 * GROUP_W + p * LANES: 2 * GROUP_W + (p + 1) * LANES]
            cb_beta = cb_ref[rows, (4 + 2 * p) * LANES:(5 + 2 * p) * LANES]
            cb_la = cb_ref[rows, (5 + 2 * p) * LANES:(6 + 2 * p) * LANES]
            pre = prefix(jnp.concatenate([cb_la * u_mask, cb_la], axis=1))
            kq = _dot_nt(jnp.concatenate([gk, gq], axis=0), _bd(gk, lo1))
            yield
            diff, cb_g = pre[:, :LANES], pre[:, LANES:]
            dec = jnp.exp(jnp.where(incl, diff, -jnp.inf))
            a = cb_beta * kq[:CHUNK] * jnp.where(strict, dec, 0.0)
            qkd = kq[CHUNK:] * dec
            a0 = jnp.where(blk8, a, 0.0)
            p1 = pp(a0, a0)
            yield
            tn = eye - a0
            tn = tn + pp(tn, p1)
            p2 = pp(p1, p1)
            yield
            tn = tn + pp(tn, p2)
            yield
            for inside, outside in ((blk16, blk8), (blk32, blk16), (None, blk32)):
                off = jnp.where(outside, 0.0, a) if inside is None else jnp.where(inside & ~outside, a, 0.0)
                m_off = pp(off, tn)
                yield
                tn = tn - pp(tn, m_off)
                yield
            eg = jnp.exp(cb_g)
            rhs = jnp.concatenate([cb_beta * gv, cb_beta * eg * gk], axis=1)
            sol = _dot(tn, _bd(rhs, lo2))
            yield
            u_c, wk_c = sol[:, :LANES], sol[:, LANES:]
            g_last = cb_g[CHUNK - 1:CHUNK, :]
            kg = gk * jnp.exp(g_last - cb_g)
            st = gs_ref[bb, p]
            st_bd = _bd(st, lo1)
            w_c = u_c - _dot(wk_c, st_bd)
            yield
            hg_ref[rows, ln] = _dot(jnp.concatenate([gq * eg, qkd], axis=1),
                                    jnp.concatenate([st_bd, _bd(w_c, lo1)], axis=0))
            gs_ref[bb, p] = jnp.exp(g_last) * st + _fold(_dot(kg.T, w_c), lo1)

        live = [gdn_instance(b, p) for b in range(NB) for p in range(2)]
        live += [mlstm_instance(b, p) for b in range(NB) for p in range(2)]
        done = object()
        n_late = len(pending)
        stage = 0
        conv_done = CONV_EARLY
        while live:
            live = [g for g in live if next(g, done) is not done]
            stage += 1
            while conv_done < min(NB, CONV_EARLY + stage * (NB - CONV_EARLY) // SPREAD_STAGES):
                next(conv)
                conv_done += 1
            emit_projection(min(n_late, stage * n_late // SPREAD_STAGES) - (n_late - len(pending)))
        for _ in conv:
            pass
        emit_projection(len(pending))

        hm = hm_ref[...]
        hm = hm * lax.rsqrt(_group_mean_sq(hm, ones4) + EPS) * mng_ref[...]
        y_ref[:, 0:GROUP_W] = (hm * jax.nn.sigmoid(sect(S_MO))).astype(bf16)
        hg = hg_ref[...]
        hg = hg * lax.rsqrt(_group_mean_sq(hg, ones4) + EPS) * gng_ref[...]
        y_ref[:, GROUP_W:2 * GROUP_W] = (hg * _silu(sect(S_GZ))).astype(bf16)
        part = dotf(y_ref[:, 0:2 * GROUP_W], wout_ref[0:2 * GROUP_W, :])
        o_ref[b_off:b_off + NB] += part.reshape(NB, CHUNK, D_MODEL)

    for k in range(N_SUB):
        substep(k, batch_rows(xc_ref, k + 1) if k + 1 < N_SUB else batch_rows(xn_ref, 0))


def _const_spec(shape):
    return pl.BlockSpec(shape, lambda *_: (0,) * len(shape))


def _layer_spec(arr, layer, single_buffer=False):
    mode = dict(pipeline_mode=pl.Buffered(1)) if single_buffer else {}
    return pl.BlockSpec((None,) + arr.shape[1:], lambda *_: (layer, 0, 0), **mode)


def _mixer_call(x, layer, sel, n1g, win, gvec, mng, gcw, gng, cdw, cvec, scw, wout):
    bsz, seq, _ = x.shape
    n_t = seq // CHUNK
    grid = (bsz // GB, n_t)
    xspec = pl.BlockSpec((GB, CHUNK, D_MODEL), lambda g, t: (g, t, 0))
    xnext = pl.BlockSpec((NB, CHUNK, D_MODEL), lambda g, t: (g * N_SUB, jnp.minimum(t + 1, n_t - 1), 0))
    lspec = functools.partial(_layer_spec, layer=layer)
    in_specs = [xspec, xnext, lspec(n1g), lspec(win, single_buffer=True), lspec(gvec), _const_spec(sel.shape),
                lspec(mng), lspec(gcw), lspec(gng), lspec(cdw), lspec(cvec), lspec(scw),
                lspec(wout, single_buffer=True)]
    per_sub = [
        pltpu.VMEM((ROWS, PROJ_PAD), f32),
        pltpu.VMEM((ROWS, D_MODEL), bf16),
        pltpu.VMEM((ROWS, N_CB * LANES), f32),
        pltpu.VMEM((ROWS, 3 * GROUP_W), f32),
        pltpu.VMEM((ROWS, D_MODEL), bf16),
        pltpu.VMEM((ROWS, GROUP_W), f32),
        pltpu.VMEM((ROWS, GROUP_W), f32),
    ]
    assert len(per_sub) == N_SUB_BUFS
    shared = [
        pltpu.VMEM((GDN_CONV - 1, GB, CHUNK + SUBLANES, 3 * GROUP_W), f32),
        pltpu.VMEM((SUBLANES, GB, CONF_HALO + CHUNK, GROUP_W), f32),
        pltpu.VMEM((SC_CONV - 1, GB, CHUNK + SUBLANES, GROUP_W), f32),
        pltpu.VMEM((GB, 2, CHUNK, 2 * LANES), f32),
        pltpu.VMEM((GB, 2, SUBLANES, LANES), f32),
        pltpu.VMEM((GB, 2, CHUNK, LANES), f32),
    ]
    return pl.pallas_call(
        _mixer_kernel,
        grid=grid,
        in_specs=in_specs,
        out_specs=xspec,
        out_shape=jax.ShapeDtypeStruct(x.shape, f32),
        scratch_shapes=per_sub * N_SUB + shared,
        compiler_params=pltpu.CompilerParams(dimension_semantics=("arbitrary", "arbitrary"),
                                             vmem_limit_bytes=VMEM_LIMIT),
        name="mixer",
    )(x, x, n1g, win, gvec, sel, mng, gcw, gng, cdw, cvec, scw, wout)


def _ffn_kernel(x_ref, g_ref, wgu_ref, wd_ref, fg_ref, o_ref, *, final):
    x = x_ref[...]
    ms = jnp.mean(x * x, axis=-1, keepdims=True)
    h = (x * lax.rsqrt(ms + EPS) * g_ref[...]).astype(bf16)
    acc = x
    for a, b in FF_CHUNKS:
        gate = jnp.dot(h, wgu_ref[:, a:b], preferred_element_type=f32)
        up = jnp.dot(h, wgu_ref[:, D_FF + a:D_FF + b], preferred_element_type=f32)
        act = (_silu(gate) * up).astype(bf16)
        acc = acc + jnp.dot(act, wd_ref[a:b, :], preferred_element_type=f32)
    if final:
        ms = jnp.mean(acc * acc, axis=-1, keepdims=True)
        acc = acc * lax.rsqrt(ms + EPS) * fg_ref[...]
    o_ref[...] = acc


def _ffn_call(x2d, layer, g, wgu, wd, fg, final):
    n = x2d.shape[0]
    xspec = pl.BlockSpec((TM, D_MODEL), lambda i: (i, 0))
    return pl.pallas_call(
        functools.partial(_ffn_kernel, final=final),
        grid=(n // TM,),
        in_specs=[xspec, _layer_spec(g, layer), _layer_spec(wgu, layer, single_buffer=True),
                  _layer_spec(wd, layer, single_buffer=True), _const_spec(fg.shape)],
        out_specs=xspec,
        out_shape=jax.ShapeDtypeStruct(x2d.shape, f32),
        compiler_params=pltpu.CompilerParams(dimension_semantics=("arbitrary",), vmem_limit_bytes=VMEM_LIMIT),
        name="ffn_final" if final else "ffn",
    )(x2d, g, wgu, wd, fg)


def _sel_matrix():
    sel = np.zeros((LANES, N_CB * LANES), np.float32)
    for blk, (kind, p) in enumerate([(0, 0), (1, 0), (0, 1), (1, 1), (2, 0), (3, 0), (2, 1), (3, 1)]):
        for half in range(2):
            src = 4 * kind + 2 * p + half
            for rep in range(3):
                sel[rep * GATE_REP + src, blk * LANES + half * CHUNK: blk * LANES + (half + 1) * CHUNK] = 1.0
    return jnp.asarray(sel, bf16)


def _pad_rows(a, rows):
    return jnp.pad(a, ((0, 0), (0, rows - a.shape[1]), (0, 0)))


def _prep_w_in(w):
    o = np.cumsum([0, 256, 256, 256, 256, 4, 4, 256, 256, 256, 256, 4, 4, 256, 256, 256, 256, 256])
    w = w.astype(bf16)
    col = lambda i: w[:, :, o[i]:o[i + 1]]
    gates = [col(4), col(5), col(10), col(11)]
    zeros = jnp.zeros(w.shape[:2] + (LANES - 3 * GATE_REP,), bf16)
    sections = [col(6), col(7), col(8),
                col(0) * (HEAD_DIM ** -0.5), col(1), col(2), col(3),
                col(9), col(12), col(13), col(14), col(15), col(16)]
    return jnp.concatenate(sections + gates * 3 + [zeros], axis=2)


def _gate_vec(igate_b, fgate_b, dt_bias, a_log):
    z4 = jnp.zeros_like(igate_b)
    bias = jnp.concatenate([igate_b, fgate_b, z4, dt_bias], axis=1)
    alog = jnp.concatenate([z4, z4, z4, a_log], axis=1)
    pad = jnp.zeros((igate_b.shape[0], LANES - 3 * GATE_REP), f32)
    rows = [jnp.concatenate([v, v, v, pad], axis=1) for v in (bias, alog)]
    return _pad_rows(jnp.stack(rows, axis=1), SUBLANES)


def kernel(x, norm1_g, w_in, m_igate_b, m_fgate_b, m_norm_g, g_conv_w, g_a_log, g_dt_bias, g_norm_g,
           c_dw_w, c_dw_b, c_ln_g, c_ln_b, s_conv_w, w_out, norm2_g, w_gate_up, w_down, final_norm_g):
    bsz, seq, d = x.shape
    depth = w_in.shape[0]
    sel = _sel_matrix()
    mixer_params = (
        norm1_g[:, None, :], _prep_w_in(w_in), _gate_vec(m_igate_b, m_fgate_b, g_dt_bias, g_a_log),
        m_norm_g[:, None, :], _pad_rows(g_conv_w, SUBLANES), jnp.tile(g_norm_g, (1, N_HEADS))[:, None, :],
        _pad_rows(c_dw_w, CONF_HALO), _pad_rows(jnp.stack([c_dw_b, c_ln_g, c_ln_b], axis=1), SUBLANES),
        _pad_rows(s_conv_w, SUBLANES), w_out.astype(bf16))
    ffn_params = (norm2_g[:, None, :], w_gate_up.astype(bf16), w_down.astype(bf16), final_norm_g.reshape(1, d))
    for l in range(depth):
        x = _mixer_call(x, l, sel, *mixer_params)
        x = _ffn_call(x.reshape(bsz * seq, d), l, *ffn_params, final=(l == depth - 1)).reshape(bsz, seq, d)
    return x
```

```python
import functools

import numpy as np
import jax
import jax.numpy as jnp
from jax import lax
from jax.experimental import pallas as pl
from jax.experimental.pallas import tpu as pltpu

D_MODEL = 1024
GROUP_W = 256
HEAD_DIM = 64
N_HEADS = 4
CHUNK = 64
GDN_CONV = 4
CONF_CONV = 31
SC_CONV = 3
D_FF = 2816
EPS = 1e-6

LANES = 128
SUBLANES = 8
N_SECT = 13
GATE_OFF = N_SECT * GROUP_W
PROJ_PAD = GATE_OFF + LANES
GATE_REP = 16
N_CB = 8

S_GQ, S_GK, S_GV, S_MQ, S_MK, S_MV, S_MO, S_GZ, S_CA, S_CG, S_SB, S_SC, S_SX = range(13)

NB = 4
N_SUB = 2
GB = NB * N_SUB
ROWS = NB * CHUNK
PROJ_STEP = 2 * LANES
CONF_HALO = 32
PROJ_EARLY = 4
CONV_EARLY = 2
SPREAD_STAGES = 12
TM = 1024
FF_CHUNKS = ((0, 1024), (1024, 2048), (2048, 2816))
VMEM_LIMIT = 60 * 1024 * 1024

f32 = jnp.float32
bf16 = jnp.bfloat16


def _dot(a, b):
    return jnp.dot(a.astype(bf16), b.astype(bf16), preferred_element_type=f32)


def _dot_nt(a, b):
    return lax.dot_general(a.astype(bf16), b.astype(bf16), (((1,), (1,)), ((), ())), preferred_element_type=f32)


def _split3(x):
    hi = x.astype(bf16)
    r = x - hi.astype(f32)
    mid = r.astype(bf16)
    lo = (r - mid.astype(f32)).astype(bf16)
    return hi, mid, lo


def _softplus(x):
    return jnp.maximum(x, 0.0) + jnp.log1p(jnp.exp(-jnp.abs(x)))


def _silu(x):
    return x * jax.nn.sigmoid(x)


def _bd(y, lo_mask):
    return jnp.concatenate([jnp.where(lo_mask, y, 0.0), jnp.where(lo_mask, 0.0, y)], axis=0)


def _fold(z, lo_mask):
    return jnp.where(lo_mask, z[:CHUNK], z[CHUNK:])


def _cummax_rows(x, row):
    k = 1
    while k < CHUNK:
        x = jnp.maximum(x, jnp.where(row >= k, pltpu.roll(x, k, 0), -jnp.inf))
        k *= 2
    return x


def _group_mean_sq(x, ones_blk):
    return jnp.dot((x * x).astype(bf16), ones_blk, preferred_element_type=f32) * (1.0 / HEAD_DIM)


def _shifted_history_taps(cur, sh_ref, b, w_ref, n_taps, lanes):
    acc = cur[:, lanes] * w_ref[n_taps - 1:n_taps, lanes]
    for j in range(1, n_taps):
        acc = acc + sh_ref[j - 1, b, 0:CHUNK, lanes] * w_ref[n_taps - 1 - j:n_taps - j, lanes]
    return acc


def _store_shifted(cur, sh_ref, b, n_shift):
    for j in range(1, n_shift + 1):
        sh_ref[j - 1, b, j:CHUNK + j, :] = cur


def _roll_history(sh_ref, b, n_shift):
    for j in range(n_shift):
        sh_ref[j, b, 0:SUBLANES, :] = sh_ref[j, b, CHUNK:CHUNK + SUBLANES, :]


N_SUB_BUFS = 7


def _mixer_kernel(xc_ref, xn_ref, n1g_ref, win_ref, gvec_ref, sel_ref, mng_ref, gcw_ref, gng_ref, cdw_ref, cvec_ref,
                  scw_ref, wout_ref, o_ref, *scratch):
    sub_bufs = [scratch[k * N_SUB_BUFS:(k + 1) * N_SUB_BUFS] for k in range(N_SUB)]
    gsh_ref, ush_ref, cxsh_ref, mc_ref, mm_ref, gs_ref = scratch[N_SUB * N_SUB_BUFS:]
    t = pl.program_id(1)
    dotf = functools.partial(jnp.dot, preferred_element_type=f32)

    def norm_rows(x, h_ref):
        ms = jnp.mean(x * x, axis=-1, keepdims=True)
        h_ref[...] = (x * lax.rsqrt(ms + EPS) * n1g_ref[...]).astype(bf16)

    def batch_rows(x_ref, k):
        return x_ref[k * NB:(k + 1) * NB].reshape(ROWS, D_MODEL)

    @pl.when(t == 0)
    def _():
        gsh_ref[...] = jnp.zeros_like(gsh_ref)
        ush_ref[...] = jnp.zeros_like(ush_ref)
        cxsh_ref[...] = jnp.zeros_like(cxsh_ref)
        mc_ref[...] = jnp.zeros_like(mc_ref)
        mm_ref[...] = jnp.zeros_like(mm_ref)
        gs_ref[...] = jnp.zeros_like(gs_ref)
        p0_ref, h0_ref = sub_bufs[0][:2]
        norm_rows(batch_rows(xc_ref, 0), h0_ref)
        p0_ref[...] = dotf(h0_ref[...], win_ref[...])

    lane1 = lax.broadcasted_iota(jnp.int32, (1, LANES), 1)
    gsel = lane1 % GATE_REP
    lane256 = lax.broadcasted_iota(jnp.int32, (GROUP_W, GROUP_W), 1) // HEAD_DIM
    row256 = lax.broadcasted_iota(jnp.int32, (GROUP_W, GROUP_W), 0) // HEAD_DIM
    ones4 = (lane256 == row256).astype(bf16)
    row = lax.broadcasted_iota(jnp.int32, (CHUNK, LANES), 0)
    lane = lax.broadcasted_iota(jnp.int32, (CHUNK, LANES), 1)
    s_idx = lane % CHUNK
    lo1 = lane < CHUNK
    lo2 = (lax.broadcasted_iota(jnp.int32, (CHUNK, 2 * LANES), 1) % LANES) < CHUNK
    incl = s_idx <= row
    strict = s_idx < row
    u_mask = (row > s_idx).astype(f32)
    eye = (row == s_idx).astype(f32)
    same = lambda n: (row // n) == (s_idx // n)
    blk8, blk16, blk32 = same(8), same(16), same(32)
    r3 = lax.broadcasted_iota(jnp.int32, (CHUNK, 3 * CHUNK), 0)
    c3 = lax.broadcasted_iota(jnp.int32, (CHUNK, 3 * CHUNK), 1) % CHUNK
    tri3 = (c3 <= r3).astype(bf16)
    ones_blk = jnp.ones((CHUNK, LANES), f32)
    all256 = slice(0, GROUP_W)
    first = CONF_HALO - (CONF_CONV - 1)

    def prefix(w_ext):
        hi, mid, lo = _split3(w_ext)
        return dotf(tri3, jnp.concatenate([hi, mid, lo], axis=0))

    def pp(a, b_):
        return _dot(a, _bd(b_, lo1))

    def substep(k, x_next):
        cur, _, cb_ref, gqkv_ref, y_ref, hm_ref, hg_ref = sub_bufs[k]
        nxt, h_ref = sub_bufs[(k + 1) % N_SUB][:2]
        b_off = k * NB

        norm_rows(x_next, h_ref)

        def project_cols(c0):
            c1 = min(c0 + PROJ_STEP, PROJ_PAD)
            nxt[:, c0:c1] = dotf(h_ref[...], win_ref[:, c0:c1])

        pending = [functools.partial(project_cols, c0) for c0 in range(0, PROJ_PAD, PROJ_STEP)]

        def emit_projection(n=1):
            for _ in range(min(n, len(pending))):
                pending.pop(0)()

        def sect(s, rows=slice(None)):
            return cur[rows, s * GROUP_W:(s + 1) * GROUP_W]

        def conv_mixers():
            for b in range(NB):
                bb = b_off + b
                rows = slice(b * CHUNK, (b + 1) * CHUNK)
                u = sect(S_CA, rows) * jax.nn.sigmoid(sect(S_CG, rows))
                for r in range(SUBLANES):
                    ush_ref[r, bb, CONF_HALO - r:CONF_HALO - r + CHUNK, :] = u
                acc = jnp.broadcast_to(cvec_ref[0:1, :], (CHUNK, GROUP_W))
                for o in range(first, first + CONF_CONV):
                    r = o % SUBLANES
                    acc = acc + ush_ref[r, bb, o - r:o - r + CHUNK, :] * cdw_ref[o - first:o - first + 1, :]
                for r in range(SUBLANES):
                    ush_ref[r, bb, 0:CONF_HALO, :] = ush_ref[r, bb, CHUNK:CHUNK + CONF_HALO, :]
                mu = jnp.mean(acc, axis=-1, keepdims=True)
                xc = acc - mu
                var = jnp.mean(xc * xc, axis=-1, keepdims=True)
                yc = xc * lax.rsqrt(var + EPS) * cvec_ref[1:2, :] + cvec_ref[2:3, :]
                y_ref[rows, 2 * GROUP_W:3 * GROUP_W] = _silu(yc).astype(bf16)

                cx = sect(S_SC, rows) * sect(S_SX, rows)
                _store_shifted(cx, cxsh_ref, bb, SC_CONV - 1)
                sconv = _shifted_history_taps(cx, cxsh_ref, bb, scw_ref, SC_CONV, all256)
                _roll_history(cxsh_ref, bb, SC_CONV - 1)
                y_ref[rows, 3 * GROUP_W:] = (sect(S_SB, rows) * sconv).astype(bf16)
                if b == NB - 1:
                    part = dotf(y_ref[:, 2 * GROUP_W:], wout_ref[2 * GROUP_W:, :])
                    o_ref[b_off:b_off + NB] = (batch_rows(xc_ref, k) + part).reshape(NB, CHUNK, D_MODEL)
                yield

        conv = conv_mixers()
        for _ in range(CONV_EARLY):
            next(conv)
            emit_projection(2)

        g_pre = cur[:, GATE_OFF:] + gvec_ref[0:1, :]
        a_coef = -jnp.exp(gvec_ref[1:2, :])
        gval = jnp.where(gsel < 4, g_pre,
                         jnp.where(gsel < 8, -_softplus(-g_pre),
                                   jnp.where(gsel < 12, jax.nn.sigmoid(g_pre), a_coef * _softplus(g_pre))))
        g_hi, g_mid, g_lo = _split3(gval)
        pieces = jnp.where(lane1 < GATE_REP, g_hi, jnp.where(lane1 < 2 * GATE_REP, g_mid, g_lo))
        cb_ref[...] = dotf(pieces, sel_ref[...])
        emit_projection()

        for b in range(NB):
            bb = b_off + b
            rows = slice(b * CHUNK, (b + 1) * CHUNK)
            raw = cur[rows, 0:3 * GROUP_W]
            _store_shifted(raw, gsh_ref, bb, GDN_CONV - 1)
            for j in range(3):
                lanes = slice(j * GROUP_W, (j + 1) * GROUP_W)
                v = _silu(_shifted_history_taps(raw, gsh_ref, bb, gcw_ref, GDN_CONV, lanes))
                if j < 2:
                    v = v * lax.rsqrt(_group_mean_sq(v, ones4) * HEAD_DIM + EPS)
                    if j == 0:
                        v = v * (HEAD_DIM ** -0.5)
                gqkv_ref[rows, lanes] = v
            _roll_history(gsh_ref, bb, GDN_CONV - 1)
            emit_projection(PROJ_EARLY * (b + 1) // NB - PROJ_EARLY * b // NB)

        def mlstm_instance(b, p):
            bb = b_off + b
            rows = slice(b * CHUNK, (b + 1) * CHUNK)
            ln = slice(p * LANES, (p + 1) * LANES)
            q, k_, v = (cur[rows, s * GROUP_W + p * LANES:s * GROUP_W + (p + 1) * LANES]
                        for s in (S_MQ, S_MK, S_MV))
            cb_i = cb_ref[rows, (2 * p) * LANES:(2 * p + 1) * LANES]
            cb_lf = cb_ref[rows, (2 * p + 1) * LANES:(2 * p + 2) * LANES]
            pre = prefix(jnp.concatenate([cb_lf * u_mask + cb_i * eye, cb_lf], axis=1))
            qk = _dot_nt(q, _bd(k_, lo1))
            yield
            log_d, cb_b = pre[:, :LANES], pre[:, LANES:]
            m_prev = mm_ref[bb, p, 0:1, :]
            mmax = jnp.maximum(m_prev, _cummax_rows(cb_i - cb_b, row))
            m_t = cb_b + mmax
            s = qk * jnp.exp(jnp.where(incl, log_d - m_t, -jnp.inf))
            inter = jnp.exp(m_prev - mmax)
            v_ext = jnp.concatenate([v, ones_blk], axis=1)
            c_ext = mc_ref[bb, p]
            res = _dot(jnp.concatenate([s, inter * q], axis=1),
                       jnp.concatenate([_bd(v_ext, lo2), _bd(c_ext, lo2)], axis=0))
            m_new = m_t[CHUNK - 1:CHUNK, :]
            b_last = cb_b[CHUNK - 1:CHUNK, :]
            w = jnp.exp(b_last - cb_b + cb_i - m_new)
            decay = jnp.exp(b_last + m_prev - m_new)
            upd = _dot((k_ * w).T, v_ext)
            yield
            num, den = res[:, :LANES], res[:, LANES:]
            hm_ref[rows, ln] = num / jnp.maximum(jnp.abs(den), jnp.exp(-m_t))
            mc_ref[bb, p] = jnp.concatenate([decay, decay], axis=1) * c_ext + _fold(upd, lo2)
            mm_ref[bb, p] = jnp.broadcast_to(m_new, (SUBLANES, LANES))

        def gdn_instance(b, p):
            bb = b_off + b
            rows = slice(b * CHUNK, (b + 1) * CHUNK)
            ln = slice(p * LANES, (p + 1) * LANES)
            gq = gqkv_ref[rows, p * LANES:(p + 1) * LANES]
            gk = gqkv_ref[rows, GROUP_W + p * LANES: GROUP_W + (p + 1) * LANES]
            gv = gqkv_ref[rows, 2 * GROUP_W + p * LANES: 2 * GROUP_W + (p + 1) * LANES]
            cb_beta = cb_ref[rows, (4 + 2 * p) * LANES:(5 + 2 * p) * LANES]
            cb_la = cb_ref[rows, (5 + 2 * p) * LANES:(6 + 2 * p) * LANES]
            pre = prefix(jnp.concatenate([cb_la * u_mask, cb_la], axis=1))
            kq = _dot_nt(jnp.concatenate([gk, gq], axis=0), _bd(gk, lo1))
            yield
            diff, cb_g = pre[:, :LANES], pre[:, LANES:]
            dec = jnp.exp(jnp.where(incl, diff, -jnp.inf))
            a = cb_beta * kq[:CHUNK] * jnp.where(strict, dec, 0.0)
            qkd = kq[CHUNK:] * dec
            a0 = jnp.where(blk8, a, 0.0)
            p1 = pp(a0, a0)
            yield
            tn = eye - a0
            tn = tn + pp(tn, p1)
            p2 = pp(p1, p1)
            yield
            tn = tn + pp(tn, p2)
            yield
            for inside, outside in ((blk16, blk8), (blk32, blk16), (None, blk32)):
                off = jnp.where(outside, 0.0, a) if inside is None else jnp.where(inside & ~outside, a, 0.0)
                m_off = pp(off, tn)
                yield
                tn = tn - pp(tn, m_off)
                yield
            eg = jnp.exp(cb_g)
            rhs = jnp.concatenate([cb_beta * gv, cb_beta * eg * gk], axis=1)
            sol = _dot(tn, _bd(rhs, lo2))
            yield
            u_c, wk_c = sol[:, :LANES], sol[:, LANES:]
            g_last = cb_g[CHUNK - 1:CHUNK, :]
            kg = gk * jnp.exp(g_last - cb_g)
            st = gs_ref[bb, p]
            st_bd = _bd(st, lo1)
            w_c = u_c - _dot(wk_c, st_bd)
            yield
            hg_ref[rows, ln] = _dot(jnp.concatenate([gq * eg, qkd], axis=1),
                                    jnp.concatenate([st_bd, _bd(w_c, lo1)], axis=0))
            gs_ref[bb, p] = jnp.exp(g_last) * st + _fold(_dot(kg.T, w_c), lo1)

        live = [gdn_instance(b, p) for b in range(NB) for p in range(2)]
        live += [mlstm_instance(b, p) for b in range(NB) for p in range(2)]
        done = object()
        n_late = len(pending)
        stage = 0
        conv_done = CONV_EARLY
        while live:
            live = [g for g in live if next(g, done) is not done]
            stage += 1
            while conv_done < min(NB, CONV_EARLY + stage * (NB - CONV_EARLY) // SPREAD_STAGES):
                next(conv)
                conv_done += 1
            emit_projection(min(n_late, stage * n_late // SPREAD_STAGES) - (n_late - len(pending)))
        for _ in conv:
            pass
        emit_projection(len(pending))

        hm = hm_ref[...]
        hm = hm * lax.rsqrt(_group_mean_sq(hm, ones4) + EPS) * mng_ref[...]
        y_ref[:, 0:GROUP_W] = (hm * jax.nn.sigmoid(sect(S_MO))).astype(bf16)
        hg = hg_ref[...]
        hg = hg * lax.rsqrt(_group_mean_sq(hg, ones4) + EPS) * gng_ref[...]
        y_ref[:, GROUP_W:2 * GROUP_W] = (hg * _silu(sect(S_GZ))).astype(bf16)
        part = dotf(y_ref[:, 0:2 * GROUP_W], wout_ref[0:2 * GROUP_W, :])
        o_ref[b_off:b_off + NB] += part.reshape(NB, CHUNK, D_MODEL)

    for k in range(N_SUB):
        substep(k, batch_rows(xc_ref, k + 1) if k + 1 < N_SUB else batch_rows(xn_ref, 0))


def _const_spec(shape):
    return pl.BlockSpec(shape, lambda *_: (0,) * len(shape))


def _layer_spec(arr, layer, single_buffer=False):
    mode = dict(pipeline_mode=pl.Buffered(1)) if single_buffer else {}
    return pl.BlockSpec((None,) + arr.shape[1:], lambda *_: (layer, 0, 0), **mode)


def _mixer_call(x, layer, sel, n1g, win, gvec, mng, gcw, gng, cdw, cvec, scw, wout):
    bsz, seq, _ = x.shape
    n_t = seq // CHUNK
    grid = (bsz // GB, n_t)
    xspec = pl.BlockSpec((GB, CHUNK, D_MODEL), lambda g, t: (g, t, 0))
    xnext = pl.BlockSpec((NB, CHUNK, D_MODEL), lambda g, t: (g * N_SUB, jnp.minimum(t + 1, n_t - 1), 0))
    lspec = functools.partial(_layer_spec, layer=layer)
    in_specs = [xspec, xnext, lspec(n1g), lspec(win, single_buffer=True), lspec(gvec), _const_spec(sel.shape),
                lspec(mng), lspec(gcw), lspec(gng), lspec(cdw), lspec(cvec), lspec(scw),
                lspec(wout, single_buffer=True)]
    per_sub = [
        pltpu.VMEM((ROWS, PROJ_PAD), f32),
        pltpu.VMEM((ROWS, D_MODEL), bf16),
        pltpu.VMEM((ROWS, N_CB * LANES), f32),
        pltpu.VMEM((ROWS, 3 * GROUP_W), f32),
        pltpu.VMEM((ROWS, D_MODEL), bf16),
        pltpu.VMEM((ROWS, GROUP_W), f32),
        pltpu.VMEM((ROWS, GROUP_W), f32),
    ]
    assert len(per_sub) == N_SUB_BUFS
    shared = [
        pltpu.VMEM((GDN_CONV - 1, GB, CHUNK + SUBLANES, 3 * GROUP_W), f32),
        pltpu.VMEM((SUBLANES, GB, CONF_HALO + CHUNK, GROUP_W), f32),
        pltpu.VMEM((SC_CONV - 1, GB, CHUNK + SUBLANES, GROUP_W), f32),
        pltpu.VMEM((GB, 2, CHUNK, 2 * LANES), f32),
        pltpu.VMEM((GB, 2, SUBLANES, LANES), f32),
        pltpu.VMEM((GB, 2, CHUNK, LANES), f32),
    ]
    return pl.pallas_call(
        _mixer_kernel,
        grid=grid,
        in_specs=in_specs,
        out_specs=xspec,
        out_shape=jax.ShapeDtypeStruct(x.shape, f32),
        scratch_shapes=per_sub * N_SUB + shared,
        compiler_params=pltpu.CompilerParams(dimension_semantics=("arbitrary", "arbitrary"),
                                             vmem_limit_bytes=VMEM_LIMIT),
        name="mixer",
    )(x, x, n1g, win, gvec, sel, mng, gcw, gng, cdw, cvec, scw, wout)


def _ffn_kernel(x_ref, g_ref, wgu_ref, wd_ref, fg_ref, o_ref, *, final):
    x = x_ref[...]
    ms = jnp.mean(x * x, axis=-1, keepdims=True)
    h = (x * lax.rsqrt(ms + EPS) * g_ref[...]).astype(bf16)
    acc = None
    for a, b in FF_CHUNKS:
        gate = jnp.dot(h, wgu_ref[:, a:b], preferred_element_type=f32)
        up = jnp.dot(h, wgu_ref[:, D_FF + a:D_FF + b], preferred_element_type=f32)
        act = (_silu(gate) * up).astype(bf16)
        down = jnp.dot(act, wd_ref[a:b, :], preferred_element_type=f32)
        acc = down if acc is None else acc + down
    acc = acc + x_ref[...]
    if final:
        ms = jnp.mean(acc * acc, axis=-1, keepdims=True)
        acc = acc * lax.rsqrt(ms + EPS) * fg_ref[...]
    o_ref[...] = acc


def _ffn_call(x2d, layer, g, wgu, wd, fg, final):
    n = x2d.shape[0]
    xspec = pl.BlockSpec((TM, D_MODEL), lambda i: (i, 0))
    return pl.pallas_call(
        functools.partial(_ffn_kernel, final=final),
        grid=(n // TM,),
        in_specs=[xspec, _layer_spec(g, layer), _layer_spec(wgu, layer, single_buffer=True),
                  _layer_spec(wd, layer, single_buffer=True), _const_spec(fg.shape)],
        out_specs=xspec,
        out_shape=jax.ShapeDtypeStruct(x2d.shape, f32),
        compiler_params=pltpu.CompilerParams(dimension_semantics=("arbitrary",), vmem_limit_bytes=VMEM_LIMIT),
        name="ffn_final" if final else "ffn",
    )(x2d, g, wgu, wd, fg)


def _sel_matrix():
    sel = np.zeros((LANES, N_CB * LANES), np.float32)
    for blk, (kind, p) in enumerate([(0, 0), (1, 0), (0, 1), (1, 1), (2, 0), (3, 0), (2, 1), (3, 1)]):
        for half in range(2):
            src = 4 * kind + 2 * p + half
            for rep in range(3):
                sel[rep * GATE_REP + src, blk * LANES + half * CHUNK: blk * LANES + (half + 1) * CHUNK] = 1.0
    return jnp.asarray(sel, bf16)


def _pad_rows(a, rows):
    return jnp.pad(a, ((0, 0), (0, rows - a.shape[1]), (0, 0)))


def _prep_w_in(w):
    o = np.cumsum([0, 256, 256, 256, 256, 4, 4, 256, 256, 256, 256, 4, 4, 256, 256, 256, 256, 256])
    w = w.astype(bf16)
    col = lambda i: w[:, :, o[i]:o[i + 1]]
    gates = [col(4), col(5), col(10), col(11)]
    zeros = jnp.zeros(w.shape[:2] + (LANES - 3 * GATE_REP,), bf16)
    sections = [col(6), col(7), col(8),
                col(0) * (HEAD_DIM ** -0.5), col(1), col(2), col(3),
                col(9), col(12), col(13), col(14), col(15), col(16)]
    return jnp.concatenate(sections + gates * 3 + [zeros], axis=2)


def _gate_vec(igate_b, fgate_b, dt_bias, a_log):
    z4 = jnp.zeros_like(igate_b)
    bias = jnp.concatenate([igate_b, fgate_b, z4, dt_bias], axis=1)
    alog = jnp.concatenate([z4, z4, z4, a_log], axis=1)
    pad = jnp.zeros((igate_b.shape[0], LANES - 3 * GATE_REP), f32)
    rows = [jnp.concatenate([v, v, v, pad], axis=1) for v in (bias, alog)]
    return _pad_rows(jnp.stack(rows, axis=1), SUBLANES)


def kernel(x, norm1_g, w_in, m_igate_b, m_fgate_b, m_norm_g, g_conv_w, g_a_log, g_dt_bias, g_norm_g,
           c_dw_w, c_dw_b, c_ln_g, c_ln_b, s_conv_w, w_out, norm2_g, w_gate_up, w_down, final_norm_g):
    bsz, seq, d = x.shape
    depth = w_in.shape[0]
    sel = _sel_matrix()
    mixer_params = (
        norm1_g[:, None, :], _prep_w_in(w_in), _gate_vec(m_igate_b, m_fgate_b, g_dt_bias, g_a_log),
        m_norm_g[:, None, :], _pad_rows(g_conv_w, SUBLANES), jnp.tile(g_norm_g, (1, N_HEADS))[:, None, :],
        _pad_rows(c_dw_w, CONF_HALO), _pad_rows(jnp.stack([c_dw_b, c_ln_g, c_ln_b], axis=1), SUBLANES),
        _pad_rows(s_conv_w, SUBLANES), w_out.astype(bf16))
    ffn_params = (norm2_g[:, None, :], w_gate_up.astype(bf16), w_down.astype(bf16), final_norm_g.reshape(1, d))
    for l in range(depth):
        x = _mixer_call(x, l, sel, *mixer_params)
        x = _ffn_call(x.reshape(bsz * seq, d), l, *ffn_params, final=(l == depth - 1)).reshape(bsz, seq, d)
    return x
```
